```python
import math
import jax
import jax.numpy as jnp
from jax import lax
import numpy as np

D_MODEL = 4096
BATCH = 16
SEQ = 256
DEPTH = 4
DEC_BATCH = 2
DEC_SEQ = 4096
PAST_LEN = 256

GRID_W = 64
N_MIXERS = 4
N_PER_MIXER = DEPTH // N_MIXERS
DN_ALPHA = (2 * DEPTH) ** 0.25
DN_BETA = (8 * DEPTH) ** -0.25
LN_EPS = 1e-5
NEG_INF = -1e30
Q_BLOCK = 128

ATTN_HEADS = 64
ATTN_KV_HEADS = 8
ATTN_HEAD_DIM = 64
ATTN_GROUP = ATTN_HEADS // ATTN_KV_HEADS
ATTN_WINDOW = 128
ROPE_BASE = 10000.0

NA_HEADS = 32
NA_HEAD_DIM = 128
NA_ROWS = 8
NA_COLS = 16

MLSTM_HEADS = 8
MLSTM_DK = 256
MLSTM_DV = 512
MLSTM_CHUNK = 64

CONV_WIDTH = 31

N_EXPERTS = 32
TOP_K = 4
D_EXPERT = 1024
SWIGLU_LIMIT = 7.0
SWIGLU_ALPHA = 1.702
MOE_BLOCK = 128

kernel_name = 'hybrid_diffusion_interleaved_step'

F32 = jnp.float32


def _layer_norm(x, g, b):
    xf = x.astype(F32)
    mu = jnp.mean(xf, -1, keepdims=True)
    var = jnp.mean(jnp.square(xf - mu), -1, keepdims=True)
    y = (xf - mu) * lax.rsqrt(var + LN_EPS)
    return (y * g.astype(F32) + b.astype(F32)).astype(x.dtype)


def _ada(cond, w, b):
    return jnp.split(jax.nn.silu(cond) @ w + b, 6, axis=-1)


def _modulate(x, shift, scale):
    return x * (1 + scale[:, None, :]) + shift[:, None, :]


def _rot_half(xa, pos):
    n = xa.shape[-1] // 2
    inv = ROPE_BASE ** (-jnp.arange(n, dtype=F32) / n)
    ang = pos[:, None] * inv[None, :]
    cos = jnp.cos(ang)[None, :, None, :].astype(xa.dtype)
    sin = jnp.sin(ang)[None, :, None, :].astype(xa.dtype)
    x1, x2 = xa[..., :n], xa[..., n:]
    return jnp.concatenate([x1 * cos - x2 * sin, x1 * sin + x2 * cos], -1)


def _axial_rope(x):
    S, half = x.shape[1], x.shape[-1] // 2
    t = jnp.arange(S)
    row = (t // GRID_W).astype(F32)
    col = (t % GRID_W).astype(F32)
    return jnp.concatenate([_rot_half(x[..., :half], row), _rot_half(x[..., half:], col)], -1)


def _softmax_av(s, v, sink):
    if sink is None:
        p = jax.nn.softmax(s, axis=-1)
    else:
        sk = sink.astype(F32)[None, :, :, None, None]
        m = jnp.maximum(jnp.max(s, -1, keepdims=True), sk)
        e = jnp.exp(s - m)
        p = e / (jnp.sum(e, -1, keepdims=True) + jnp.exp(sk - m))
    return jnp.einsum('bhgqk,bkhd->bqhgd', p.astype(v.dtype), v)


def _dense_attn(q, k, v, sink):
    B, S, KVH, G, dh = q.shape
    nb = S // Q_BLOCK
    qb = jnp.moveaxis(q.reshape(B, nb, Q_BLOCK, KVH, G, dh), 1, 0)
    scale = dh ** -0.5

    def one(qblk):
        s = jnp.einsum('bqhgd,bkhd->bhgqk', qblk, k, preferred_element_type=F32) * scale
        return _softmax_av(s, v, sink)

    o = lax.map(one, qb)
    return jnp.moveaxis(o, 0, 1).reshape(B, S, KVH * G * dh)


def _attn_qkv(h, wqkv, rope):
    B, S, _ = h.shape
    nq = ATTN_HEADS * ATTN_HEAD_DIM
    nkv = ATTN_KV_HEADS * ATTN_HEAD_DIM
    z = h @ wqkv
    q = z[..., :nq].reshape(B, S, ATTN_HEADS, ATTN_HEAD_DIM)
    k = z[..., nq:nq + nkv].reshape(B, S, ATTN_KV_HEADS, ATTN_HEAD_DIM)
    v = z[..., nq + nkv:].reshape(B, S, ATTN_KV_HEADS, ATTN_HEAD_DIM)
    if rope:
        q = _axial_rope(q)
        k = _axial_rope(k)
    return q.reshape(B, S, ATTN_KV_HEADS, ATTN_GROUP, ATTN_HEAD_DIM), k, v


def _attn_context(h, wqkv, wo, sink):
    q, k, v = _attn_qkv(h, wqkv, False)
    o = _dense_attn(q, k, v, sink.reshape(ATTN_KV_HEADS, ATTN_GROUP))
    return o @ wo, k, v


def _attn_latent(h, wqkv, wo, sink, k_ctx, v_ctx):
    B, S, _ = h.shape
    q, k, v = _attn_qkv(h, wqkv, True)
    blk = ATTN_WINDOW
    nb = S // blk

    def band(a):
        ap = jnp.pad(a, ((0, 0), (blk, blk), (0, 0), (0, 0))).reshape(B, nb + 2, blk, ATTN_KV_HEADS, ATTN_HEAD_DIM)
        return jnp.moveaxis(jnp.concatenate([ap[:, :-2], ap[:, 1:-1], ap[:, 2:]], axis=2), 1, 0)

    kb, vb = band(k), band(v)
    qb = jnp.moveaxis(q.reshape(B, nb, blk, ATTN_KV_HEADS, ATTN_GROUP, ATTN_HEAD_DIM), 1, 0)
    bi = jnp.arange(nb)[:, None, None]
    qpos = bi * blk + jnp.arange(blk)[None, :, None]
    kpos = (bi - 1) * blk + jnp.arange(3 * blk)[None, None, :]
    mask = (jnp.abs(qpos - kpos) <= ATTN_WINDOW) & (kpos >= 0) & (kpos < S)
    sk = sink.reshape(ATTN_KV_HEADS, ATTN_GROUP)
    scale = ATTN_HEAD_DIM ** -0.5

    def one(args):
        qblk, kblk, vblk, mblk = args
        s_loc = jnp.einsum('bqhgd,bkhd->bhgqk', qblk, kblk, preferred_element_type=F32) * scale
        s_loc = jnp.where(mblk, s_loc, NEG_INF)
        s_ctx = jnp.einsum('bqhgd,bkhd->bhgqk', qblk, k_ctx.astype(qblk.dtype), preferred_element_type=F32) * scale
        vv = jnp.concatenate([vblk, v_ctx.astype(vblk.dtype)], axis=1)
        return _softmax_av(jnp.concatenate([s_loc, s_ctx], -1), vv, sk)

    o = lax.map(one, (qb, kb, vb, mask))
    return jnp.moveaxis(o, 0, 1).reshape(B, S, ATTN_HEADS * ATTN_HEAD_DIM) @ wo


def _na_qkv(h, wqkv):
    B, S, _ = h.shape
    z = (h @ wqkv).reshape(B, S, 3, NA_HEADS, NA_HEAD_DIM)
    return z[:, :, 0], z[:, :, 1], z[:, :, 2]


def _na_context(h, wqkv, wo):
    q, k, v = _na_qkv(h, wqkv)
    o = _dense_attn(q[:, :, :, None, :], k, v, None)
    return o @ wo, k, v


def _na_latent(h, wqkv, wo, rpb, k_ctx, v_ctx):
    B, S, _ = h.shape
    W = GRID_W
    R = S // W
    KR = min(NA_ROWS, R)
    KC = NA_COLS
    q, k, v = _na_qkv(h, wqkv)
    qr = jnp.moveaxis(q.reshape(B, R, W, NA_HEADS, NA_HEAD_DIM), 1, 0)
    col = jnp.arange(W)
    c0 = jnp.clip(col - KC // 2, 0, W - KC)
    kcol = jnp.arange(KR * W) % W
    krow = jnp.arange(KR * W) // W
    colmask = (kcol[None, :] >= c0[:, None]) & (kcol[None, :] < c0[:, None] + KC)
    dc_idx = jnp.clip(kcol[None, :] - col[:, None] + NA_COLS - 1, 0, 2 * NA_COLS - 2)
    scale = NA_HEAD_DIM ** -0.5

    def one(args):
        r, qblk = args
        r0 = jnp.clip(r - KR // 2, 0, R - KR)
        kblk = lax.dynamic_slice_in_dim(k, r0 * W, KR * W, axis=1)
        vblk = lax.dynamic_slice_in_dim(v, r0 * W, KR * W, axis=1)
        dr_idx = r0 + krow - r + NA_ROWS - 1
        bias = rpb[:, dr_idx[None, :], dc_idx].astype(F32)
        s_loc = jnp.einsum('bqhd,bkhd->bhqk', qblk, kblk, preferred_element_type=F32) * scale + bias
        s_loc = jnp.where(colmask, s_loc, NEG_INF)
        s_ctx = jnp.einsum('bqhd,bkhd->bhqk', qblk, k_ctx.astype(qblk.dtype), preferred_element_type=F32) * scale
        p = jax.nn.softmax(jnp.concatenate([s_loc, s_ctx], -1), axis=-1)
        vv = jnp.concatenate([vblk, v_ctx.astype(vblk.dtype)], axis=1)
        return jnp.einsum('bhqk,bkhd->bqhd', p.astype(vv.dtype), vv)

    o = lax.map(one, (jnp.arange(R), qr))
    return jnp.moveaxis(o, 0, 1).reshape(B, S, NA_HEADS * NA_HEAD_DIM) @ wo


def _mlstm_chunkwise(q, k, v, ig, lf, C0, n0, m0):
    B, NH, S, _ = q.shape
    DV = v.shape[-1]
    L = MLSTM_CHUNK
    nc = S // L

    def chunks(a):
        return jnp.moveaxis(a.reshape((B, NH, nc, L) + a.shape[3:]), 2, 0)

    tri = jnp.tril(jnp.ones((L, L), bool))

    def step(carry, xs):
        C, n, m = carry
        qc, kc, vc, ic, fc = xs
        b = jnp.cumsum(fc, -1)
        dmat = jnp.where(tri, b[..., :, None] - b[..., None, :] + ic[..., None, :], -jnp.inf)
        inter = b + m[..., None]
        mt = jnp.maximum(inter, jnp.max(dmat, -1))
        a = jnp.exp(inter - mt)
        sqk = jnp.einsum('bhtd,bhsd->bhts', qc, kc) * jnp.exp(dmat - mt[..., None])
        num = a[..., None] * jnp.einsum('bhtd,bhde->bhte', qc, C) + jnp.einsum('bhts,bhse->bhte', sqk, vc)
        den = a * jnp.einsum('bhtd,bhd->bht', qc, n) + jnp.sum(sqk, -1)
        hc = num / jnp.maximum(jnp.abs(den), jnp.exp(-mt))[..., None]
        bl = b[..., -1]
        g = bl[..., None] - b + ic
        m_new = jnp.maximum(bl + m, jnp.max(g, -1))
        kw = kc * jnp.exp(g - m_new[..., None])[..., None]
        decay = jnp.exp(bl + m - m_new)
        C_new = decay[..., None, None] * C + jnp.einsum('bhsd,bhse->bhde', kw, vc)
        n_new = decay[..., None] * n + jnp.sum(kw, 2)
        return (C_new, n_new, m_new), hc

    (C, n, m), hs = lax.scan(step, (C0, n0, m0), (chunks(q), chunks(k), chunks(v), chunks(ig), chunks(lf)))
    return jnp.moveaxis(hs, 0, 2).reshape(B, NH, S, DV), C, n, m


def _mlstm_mix(h, win, wgate, bgate, gnorm, wo, init_f, init_b):
    B, S, _ = h.shape
    nqk = MLSTM_HEADS * MLSTM_DK
    nv = MLSTM_HEADS * MLSTM_DV
    z = h @ win

    def heads(a, d):
        return a.reshape(B, S, MLSTM_HEADS, d).transpose(0, 2, 1, 3).astype(F32)

    q = heads(z[..., :nqk], MLSTM_DK) * (MLSTM_DK ** -0.5)
    k = heads(z[..., nqk:2 * nqk], MLSTM_DK)
    v = heads(z[..., 2 * nqk:2 * nqk + nv], MLSTM_DV)
    o = z[..., 2 * nqk + nv:]
    g = ((h @ wgate).astype(F32) + bgate.astype(F32)).reshape(B, S, 4, MLSTM_HEADS).transpose(2, 0, 3, 1)
    hf, Cf, nf, mf = _mlstm_chunkwise(q, k, v, g[0], jax.nn.log_sigmoid(g[1]), *init_f)
    rev = lambda a: jnp.flip(a, axis=2)
    hb, Cb, nb_, mb = _mlstm_chunkwise(rev(q), rev(k), rev(v), jnp.flip(g[2], -1),
                                       jnp.flip(jax.nn.log_sigmoid(g[3]), -1), *init_b)
    hsum = hf + rev(hb)
    hn = hsum * lax.rsqrt(jnp.mean(jnp.square(hsum), -1, keepdims=True) + LN_EPS)
    hn = hn.transpose(0, 2, 1, 3).reshape(B, S, nv) * gnorm.astype(F32)
    out = (jax.nn.sigmoid(o.astype(F32)) * hn).astype(h.dtype) @ wo
    return out, (jnp.stack([Cf, Cb], 1), jnp.stack([nf, nb_], 1), jnp.stack([mf, mb], 1))


def _conv_module(h, w1, b1, dw, dwb, lng, lnb, w2, b2):
    D = h.shape[-1]
    a, gt = jnp.split(h @ w1 + b1, 2, axis=-1)
    u = a * jax.nn.sigmoid(gt)
    u = lax.conv_general_dilated(u, dw[:, None, :].astype(u.dtype), window_strides=(1,),
                                 padding=[(CONV_WIDTH // 2, CONV_WIDTH // 2)],
                                 dimension_numbers=('NWC', 'WIO', 'NWC'), feature_group_count=D) + dwb
    u = jax.nn.silu(_layer_norm(u, lng, lnb))
    return u @ w2 + b2


def _moe(h, wr, br, w1, b1, w2, b2):
    B, S, D = h.shape
    N = B * S
    x = h.reshape(N, D)
    logits = jnp.dot(x, wr, preferred_element_type=F32) + br.astype(F32)
    top_v, top_i = lax.top_k(logits, TOP_K)
    gates = jax.nn.softmax(top_v, axis=-1)
    A = N * TOP_K
    flat_e = top_i.reshape(A)
    order = jnp.argsort(flat_e)
    e_sorted = flat_e[order]
    tok_sorted = (order // TOP_K).astype(jnp.int32)
    gate_sorted = gates.reshape(A)[order]
    counts = jnp.bincount(flat_e, length=N_EXPERTS)
    padded = (counts + MOE_BLOCK - 1) // MOE_BLOCK * MOE_BLOCK
    pad_end = jnp.cumsum(padded)
    pad_start = pad_end - padded
    grp_start = jnp.cumsum(counts) - counts
    dest = pad_start[e_sorted] + jnp.arange(A) - grp_start[e_sorted]
    n_blocks = -(-A // MOE_BLOCK) + N_EXPERTS
    slot_tok = jnp.full((n_blocks * MOE_BLOCK,), N, jnp.int32).at[dest].set(tok_sorted)
    blk_expert = jnp.minimum(jnp.searchsorted(pad_end, jnp.arange(n_blocks) * MOE_BLOCK, side='right'), N_EXPERTS - 1)
    xb = jnp.concatenate([x, jnp.zeros((1, D), x.dtype)])[slot_tok].reshape(n_blocks, MOE_BLOCK, D)

    def expert_block(args):
        xblk, e = args
        u = xblk @ w1[e] + b1[e]
        glu, lin = u[:, 0::2], u[:, 1::2]
        glu = jnp.minimum(glu, SWIGLU_LIMIT)
        lin = jnp.clip(lin, -SWIGLU_LIMIT, SWIGLU_LIMIT)
        act = glu * jax.nn.sigmoid(SWIGLU_ALPHA * glu) * (lin + 1)
        return act @ w2[e] + b2[e]

    yb = lax.map(expert_block, (xb, blk_expert)).reshape(n_blocks * MOE_BLOCK, D)
    y = yb[dest].astype(F32) * gate_sorted[:, None]
    out = jax.ops.segment_sum(y, tok_sorted, num_segments=N)
    return out.astype(h.dtype).reshape(B, S, D)


def setup_inputs(seed: int = 0) -> dict:
    key = jax.random.key(seed)
    ks = iter(jax.random.split(key, 64))
    D = D_MODEL
    NP = N_PER_MIXER
    H = MLSTM_HEADS

    def nrm(shape, std=1.0):
        return std * jax.random.normal(next(ks), shape, F32)

    def uni(shape, std):
        a = std * math.sqrt(3.0)
        return jax.random.uniform(next(ks), shape, F32, -a, a)

    inp = {}
    inp['x_prompt'] = nrm((BATCH, SEQ, D))
    inp['x_sample'] = nrm((DEC_BATCH, DEC_SEQ, D))
    inp['cache_attn_k'] = nrm((DEC_BATCH, NP, PAST_LEN, ATTN_KV_HEADS, ATTN_HEAD_DIM))
    inp['cache_attn_v'] = nrm((DEC_BATCH, NP, PAST_LEN, ATTN_KV_HEADS, ATTN_HEAD_DIM))
    inp['cache_na_k'] = nrm((DEC_BATCH, NP, PAST_LEN, NA_HEADS, NA_HEAD_DIM))
    inp['cache_na_v'] = nrm((DEC_BATCH, NP, PAST_LEN, NA_HEADS, NA_HEAD_DIM))
    inp['state_mlstm_C'] = nrm((DEC_BATCH, NP, 2, H, MLSTM_DK, MLSTM_DV), 0.1)
    inp['state_mlstm_n'] = nrm((DEC_BATCH, NP, 2, H, MLSTM_DK), 0.5)
    inp['state_mlstm_m'] = nrm((DEC_BATCH, NP, 2, H), 0.5)
    inp['c'] = nrm((DEC_BATCH, D))
    inp['c_ctx'] = nrm((D,))
    inp['ada_w'] = uni((DEPTH, D, 6 * D), 0.3 * D ** -0.5)
    inp['ada_b'] = nrm((DEPTH, 6 * D), 0.02)
    inp['ln1_g'] = 1.0 + nrm((DEPTH, D), 0.05)
    inp['ln1_b'] = nrm((DEPTH, D), 0.02)
    inp['ln2_g'] = 1.0 + nrm((DEPTH, D), 0.05)
    inp['ln2_b'] = nrm((DEPTH, D), 0.02)
    nqa = ATTN_HEADS * ATTN_HEAD_DIM
    inp['attn_wqkv'] = uni((NP, D, nqa + 2 * ATTN_KV_HEADS * ATTN_HEAD_DIM), D ** -0.5)
    inp['attn_wo'] = uni((NP, nqa, D), DN_BETA * nqa ** -0.5)
    inp['attn_sink'] = nrm((NP, ATTN_HEADS), 0.5)
    nna = NA_HEADS * NA_HEAD_DIM
    inp['na_wqkv'] = uni((NP, D, 3 * nna), D ** -0.5)
    inp['na_wo'] = uni((NP, nna, D), DN_BETA * nna ** -0.5)
    inp['na_rpb'] = nrm((NP, NA_HEADS, 2 * NA_ROWS - 1, 2 * NA_COLS - 1), 0.1)
    nv = H * MLSTM_DV
    inp['mlstm_win'] = uni((NP, D, 2 * H * MLSTM_DK + 2 * nv), D ** -0.5)
    inp['mlstm_wgate'] = uni((NP, D, 4 * H), 0.3 * D ** -0.5)
    inp['mlstm_bgate'] = jnp.concatenate([nrm((NP, H), 0.1), 3.0 + nrm((NP, H), 0.1),
                                          nrm((NP, H), 0.1), 3.0 + nrm((NP, H), 0.1)], axis=-1)
    inp['mlstm_gnorm'] = 1.0 + nrm((NP, nv), 0.05)
    inp['mlstm_wo'] = uni((NP, nv, D), DN_BETA * nv ** -0.5)
    inp['conv_w1'] = uni((NP, D, 2 * D), D ** -0.5)
    inp['conv_b1'] = nrm((NP, 2 * D), 0.02)
    inp['conv_dw'] = uni((NP, CONV_WIDTH, D), CONV_WIDTH ** -0.5)
    inp['conv_dwb'] = nrm((NP, D), 0.02)
    inp['conv_ln_g'] = 1.0 + nrm((NP, D), 0.05)
    inp['conv_ln_b'] = nrm((NP, D), 0.02)
    inp['conv_w2'] = uni((NP, D, D), DN_BETA * D ** -0.5)
    inp['conv_b2'] = nrm((NP, D), 0.02)
    inp['moe_wr'] = uni((DEPTH, D, N_EXPERTS), D ** -0.5)
    inp['moe_br'] = nrm((DEPTH, N_EXPERTS), 0.01)
    inp['moe_w1'] = uni((DEPTH, N_EXPERTS, D, 2 * D_EXPERT), D ** -0.5)
    inp['moe_b1'] = nrm((DEPTH, N_EXPERTS, 2 * D_EXPERT), 0.02)
    inp['moe_w2'] = uni((DEPTH, N_EXPERTS, D_EXPERT, D), DN_BETA * D_EXPERT ** -0.5)
    inp['moe_b2'] = nrm((DEPTH, N_EXPERTS, D), 0.02)
    return inp


def reference(x_prompt, x_sample, cache_attn_k, cache_attn_v, cache_na_k, cache_na_v,
              state_mlstm_C, state_mlstm_n, state_mlstm_m, c, c_ctx, ada_w, ada_b,
              ln1_g, ln1_b, ln2_g, ln2_b, attn_wqkv, attn_wo, attn_sink, na_wqkv, na_wo, na_rpb,
              mlstm_win, mlstm_wgate, mlstm_bgate, mlstm_gnorm, mlstm_wo,
              conv_w1, conv_b1, conv_dw, conv_dwb, conv_ln_g, conv_ln_b, conv_w2, conv_b2,
              moe_wr, moe_br, moe_w1, moe_b1, moe_w2, moe_b2):
    xp, xs = x_prompt, x_sample
    Bp = xp.shape[0]
    new_ak, new_av, new_nk, new_nv, new_C, new_n, new_m = [], [], [], [], [], [], []
    for i in range(DEPTH):
        mixer, j = i % N_MIXERS, i // N_MIXERS
        mp = _ada(c_ctx[None, :], ada_w[i], ada_b[i])
        ms = _ada(c, ada_w[i], ada_b[i])
        hp = _modulate(xp, mp[0], mp[1])
        hs = _modulate(xs, ms[0], ms[1])
        if mixer == 0:
            op, kc, vc = _attn_context(hp, attn_wqkv[j], attn_wo[j], attn_sink[j])
            os_ = _attn_latent(hs, attn_wqkv[j], attn_wo[j], attn_sink[j], cache_attn_k[:, j], cache_attn_v[:, j])
            new_ak.append(kc)
            new_av.append(vc)
        elif mixer == 1:
            op, kc, vc = _na_context(hp, na_wqkv[j], na_wo[j])
            os_ = _na_latent(hs, na_wqkv[j], na_wo[j], na_rpb[j], cache_na_k[:, j], cache_na_v[:, j])
            new_nk.append(kc)
            new_nv.append(vc)
        elif mixer == 2:
            z_init = (jnp.zeros((Bp, MLSTM_HEADS, MLSTM_DK, MLSTM_DV), F32),
                      jnp.zeros((Bp, MLSTM_HEADS, MLSTM_DK), F32),
                      jnp.zeros((Bp, MLSTM_HEADS), F32))
            op, st = _mlstm_mix(hp, mlstm_win[j], mlstm_wgate[j], mlstm_bgate[j], mlstm_gnorm[j], mlstm_wo[j],
                                z_init, z_init)
            Cc = state_mlstm_C[:, j].astype(F32)
            nc = state_mlstm_n[:, j].astype(F32)
            mc = state_mlstm_m[:, j].astype(F32)
            os_, _ = _mlstm_mix(hs, mlstm_win[j], mlstm_wgate[j], mlstm_bgate[j], mlstm_gnorm[j], mlstm_wo[j],
                                (Cc[:, 0], nc[:, 0], mc[:, 0]), (Cc[:, 1], nc[:, 1], mc[:, 1]))
            new_C.append(st[0])
            new_n.append(st[1])
            new_m.append(st[2])
        else:
            cw = (conv_w1[j], conv_b1[j], conv_dw[j], conv_dwb[j], conv_ln_g[j], conv_ln_b[j], conv_w2[j], conv_b2[j])
            op = _conv_module(hp, *cw)
            os_ = _conv_module(hs, *cw)
        xp = _layer_norm(DN_ALPHA * xp + (1 + mp[2])[:, None, :] * op, ln1_g[i], ln1_b[i])
        xs = _layer_norm(DN_ALPHA * xs + (1 + ms[2])[:, None, :] * os_, ln1_g[i], ln1_b[i])
        moe_args = (moe_wr[i], moe_br[i], moe_w1[i], moe_b1[i], moe_w2[i], moe_b2[i])
        fp = _moe(_modulate(xp, mp[3], mp[4]), *moe_args)
        fs = _moe(_modulate(xs, ms[3], ms[4]), *moe_args)
        xp = _layer_norm(DN_ALPHA * xp + (1 + mp[5])[:, None, :] * fp, ln2_g[i], ln2_b[i])
        xs = _layer_norm(DN_ALPHA * xs + (1 + ms[5])[:, None, :] * fs, ln2_g[i], ln2_b[i])
    new_attn_k = jnp.stack(new_ak, axis=1)
    new_attn_v = jnp.stack(new_av, axis=1)
    new_na_k = jnp.stack(new_nk, axis=1)
    new_na_v = jnp.stack(new_nv, axis=1)
    new_mlstm_C = jnp.stack(new_C, axis=1)
    new_mlstm_n = jnp.stack(new_n, axis=1)
    new_mlstm_m = jnp.stack(new_m, axis=1)
    return (xp, xs, new_attn_k, new_attn_v, new_na_k, new_na_v, new_mlstm_C, new_mlstm_n, new_mlstm_m)
```

```python
import functools

import jax
import jax.numpy as jnp
from jax import lax
from jax.experimental import pallas as pl
from jax.experimental.pallas import tpu as pltpu

F32 = jnp.float32
BF16 = jnp.bfloat16
I32 = jnp.int32

D_MODEL = 4096
BATCH = 16
SEQ = 256
DEPTH = 4
DEC_BATCH = 2
DEC_SEQ = 4096
PAST_LEN = 256
GRID_W = 64
DN_ALPHA = (2 * DEPTH) ** 0.25
LN_EPS = 1e-5
NEG_INF = -1e30

ATTN_HEADS = 64
ATTN_KV_HEADS = 8
ATTN_HEAD_DIM = 64
ATTN_GROUP = ATTN_HEADS // ATTN_KV_HEADS
ATTN_WINDOW = 128
ROPE_BASE = 10000.0

NA_HEADS = 32
NA_HEAD_DIM = 128
NA_ROWS = 8
NA_COLS = 16

MLSTM_HEADS = 8
MLSTM_DK = 256
MLSTM_DV = 512

CONV_WIDTH = 31

N_EXPERTS = 32
TOP_K = 4
D_EXPERT = 1024
SWIGLU_LIMIT = 7.0
SWIGLU_ALPHA = 1.702

TP = BATCH * SEQ
TS = DEC_BATCH * DEC_SEQ
T = TP + TS
SEG = DEC_SEQ
N_SEG = T // SEG
assert TP == SEG and T % SEG == 0

V7X_VMEM_LIMIT_BYTES = 56 * 2**20


def _cp(*sem):
    return pltpu.CompilerParams(dimension_semantics=sem, vmem_limit_bytes=V7X_VMEM_LIMIT_BYTES)


def _mm_body(x_ref, w_ref, b_ref, o_ref, wbf_ref):
    @pl.when(pl.program_id(1) == 0)
    def _():
        wbf_ref[...] = w_ref[...].astype(BF16)

    acc = jnp.dot(x_ref[...].astype(BF16), wbf_ref[...], preferred_element_type=F32)
    if b_ref is not None:
        acc = acc + b_ref[...]
    o_ref[...] = acc.astype(o_ref.dtype)


def _mm_body_nobias(x_ref, w_ref, o_ref, wbf_ref):
    _mm_body(x_ref, w_ref, None, o_ref, wbf_ref)


def _matmul(x, w, widx, b=None, *, out_dtype=F32, tm=1024, tn=512, name):
    M, K = x.shape
    N = w.shape[2]
    tm, tn = min(tm, M), min(tn, N)
    assert M % tm == 0 and N % tn == 0
    in_specs = [pl.BlockSpec((tm, K), lambda j, i: (i, 0)),
                pl.BlockSpec((None, K, tn), lambda j, i: (widx, 0, j))]
    args = [x, w]
    if b is not None:
        in_specs.append(pl.BlockSpec((None, 1, tn), lambda j, i: (widx, 0, j)))
        args.append(b.reshape(b.shape[0], 1, N))
    return pl.pallas_call(
        _mm_body if b is not None else _mm_body_nobias,
        out_shape=jax.ShapeDtypeStruct((M, N), out_dtype),
        grid=(N // tn, M // tm),
        in_specs=in_specs,
        out_specs=pl.BlockSpec((tm, tn), lambda j, i: (i, j)),
        scratch_shapes=[pltpu.VMEM((K, tn), BF16)],
        compiler_params=_cp("arbitrary", "arbitrary"),
        name=name)(*args)


ADA_ROWS = 16
ADA_TN = 512


def _ada_body(c_ref, w_ref, b_ref, o_ref):
    c = c_ref[...]
    s = (c * jax.nn.sigmoid(c)).astype(BF16)
    o_ref[...] = jnp.dot(s, w_ref[...].astype(BF16), preferred_element_type=F32) + b_ref[...]


def _ada(cond, ada_w, ada_b):
    L, D, N = ada_w.shape
    return pl.pallas_call(
        _ada_body,
        out_shape=jax.ShapeDtypeStruct((L, ADA_ROWS, N), F32),
        grid=(L, N // ADA_TN),
        in_specs=[pl.BlockSpec((ADA_ROWS, D), lambda l, j: (0, 0)),
                  pl.BlockSpec((None, D, ADA_TN), lambda l, j: (l, 0, j)),
                  pl.BlockSpec((None, 1, ADA_TN), lambda l, j: (l, 0, j))],
        out_specs=pl.BlockSpec((None, ADA_ROWS, ADA_TN), lambda l, j: (l, 0, j)),
        compiler_params=_cp("arbitrary", "arbitrary"),
        name="ada")(cond, ada_w, ada_b.reshape(L, 1, N))


ROW_TILE = 256


def _seg_spec():
    return pl.BlockSpec((None, 6, D_MODEL), lambda i: (i // (SEG // ROW_TILE), 0, 0))


def _modulate_body(x_ref, m_ref, h_ref):
    m = m_ref[...]
    h_ref[...] = (x_ref[...] * (1.0 + m[1:2]) + m[0:1]).astype(BF16)


def _modulate(x, mods):
    return pl.pallas_call(
        _modulate_body,
        out_shape=jax.ShapeDtypeStruct((T, D_MODEL), BF16),
        grid=(T // ROW_TILE,),
        in_specs=[pl.BlockSpec((ROW_TILE, D_MODEL), lambda i: (i, 0)), _seg_spec()],
        out_specs=pl.BlockSpec((ROW_TILE, D_MODEL), lambda i: (i, 0)),
        compiler_params=_cp("arbitrary"),
        name="modulate")(x, mods)


def _ln_mod_body(x_ref, y_ref, m_ref, g_ref, b_ref, mn_ref, xo_ref, ho_ref, *, gate_row, shift_row, h_dtype):
    m = m_ref[...]
    v = DN_ALPHA * x_ref[...] + (1.0 + m[gate_row:gate_row + 1]) * y_ref[...]
    mu = jnp.mean(v, axis=-1, keepdims=True)
    d = v - mu
    var = jnp.mean(d * d, axis=-1, keepdims=True)
    xn = d * lax.rsqrt(var + LN_EPS) * g_ref[...] + b_ref[...]
    xo_ref[...] = xn
    if ho_ref is not None:
        mn = mn_ref[...]
        ho_ref[...] = (xn * (1.0 + mn[shift_row + 1:shift_row + 2]) + mn[shift_row:shift_row + 1]).astype(h_dtype)


def _ln_body_last(x_ref, y_ref, m_ref, g_ref, b_ref, xo_ref, *, gate_row):
    _ln_mod_body(x_ref, y_ref, m_ref, g_ref, b_ref, None, xo_ref, None, gate_row=gate_row, shift_row=0, h_dtype=None)


def _ln_mod(x, y, mods, g, b, lidx, gate_row, mods_next=None, shift_row=0, h_dtype=BF16):
    row = pl.BlockSpec((ROW_TILE, D_MODEL), lambda i: (i, 0))
    vec = pl.BlockSpec((None, 1, D_MODEL), lambda i: (lidx, 0, 0))
    in_specs = [row, row, _seg_spec(), vec, vec]
    args = [x, y, mods, g.reshape(-1, 1, D_MODEL), b.reshape(-1, 1, D_MODEL)]
    if mods_next is None:
        body = functools.partial(_ln_body_last, gate_row=gate_row)
        out_shape = jax.ShapeDtypeStruct((T, D_MODEL), F32)
        out_specs = row
    else:
        body = functools.partial(_ln_mod_body, gate_row=gate_row, shift_row=shift_row, h_dtype=h_dtype)
        in_specs.append(_seg_spec())
        args.append(mods_next)
        out_shape = (jax.ShapeDtypeStruct((T, D_MODEL), F32), jax.ShapeDtypeStruct((T, D_MODEL), h_dtype))
        out_specs = (row, row)
    return pl.pallas_call(body, out_shape=out_shape, grid=(T // ROW_TILE,), in_specs=in_specs,
                          out_specs=out_specs, compiler_params=_cp("arbitrary"), name="ln_mod")(*args)


def _dot_nt(a, b):
    return lax.dot_general(a, b, (((1,), (1,)), ((), ())), preferred_element_type=F32)


def _softmax_pv(scores, values, sink_col=None):
    m = functools.reduce(jnp.maximum, [jnp.max(s, axis=-1, keepdims=True) for s in scores])
    if sink_col is not None:
        m = jnp.maximum(m, sink_col)
    den = jnp.exp(sink_col - m) if sink_col is not None else 0.0
    out = None
    for s, v in zip(scores, values):
        e = jnp.exp(s - m)
        den = den + jnp.sum(e, axis=-1, keepdims=True)
        pv = jnp.dot(e.astype(BF16), v, preferred_element_type=F32)
        out = pv if out is None else out + pv
    return out * (1.0 / den)


QB = 128
NQB = T // QB
ATTN_SCALE = ATTN_HEAD_DIM ** -0.5
NQ_COLS = ATTN_HEADS * ATTN_HEAD_DIM
NKV_COLS = ATTN_KV_HEADS * ATTN_HEAD_DIM


def _rope_tables():
    lane = jnp.arange(128)
    hd = lane % ATTN_HEAD_DIM
    n = ATTN_HEAD_DIM // 4
    inv = ROPE_BASE ** (-(hd % n).astype(F32) / n)
    t = jnp.arange(DEC_SEQ)
    row = (t // GRID_W).astype(F32)
    col = (t % GRID_W).astype(F32)
    pos = jnp.where((hd < ATTN_HEAD_DIM // 2)[None, :], row[:, None], col[:, None])
    ang = pos * inv[None, :]
    sign = jnp.where((hd % (2 * n)) < n, -1.0, 1.0).astype(F32)
    cos = jnp.concatenate([jnp.ones((TP, 128), F32)] + [jnp.cos(ang)] * DEC_BATCH)
    sin = jnp.concatenate([jnp.zeros((TP, 128), F32)] + [jnp.sin(ang) * sign[None, :]] * DEC_BATCH)
    return cos, sin


def _qkv_rope_body(z_ref, cos_ref, sin_ref, q_ref, k_ref, v_ref):
    cos = cos_ref[...]
    sin = sin_ref[...]
    lane = lax.broadcasted_iota(I32, (QB, 128), 1)
    first = (lane % 32) < 16

    def rope(x):
        xr = jnp.where(first, pltpu.roll(x, 112, 1), pltpu.roll(x, 16, 1))
        return x * cos + xr * sin

    for c in range(NQ_COLS // 128):
        y = (rope(z_ref[:, c * 128:(c + 1) * 128]) * ATTN_SCALE).astype(BF16)
        kvh, g = (2 * c) // ATTN_GROUP, (2 * c) % ATTN_GROUP
        q_ref[kvh, g] = y[:, :64]
        q_ref[kvh, g + 1] = y[:, 64:]
    for c in range(NKV_COLS // 128):
        c0 = NQ_COLS + c * 128
        y = rope(z_ref[:, c0:c0 + 128]).astype(BF16)
        k_ref[2 * c] = y[:, :64]
        k_ref[2 * c + 1] = y[:, 64:]
        c0 = NQ_COLS + NKV_COLS + c * 128
        y = z_ref[:, c0:c0 + 128].astype(BF16)
        v_ref[2 * c] = y[:, :64]
        v_ref[2 * c + 1] = y[:, 64:]


def _qkv_rope(z, cos, sin):
    ncol = z.shape[1]
    kv_shape = jax.ShapeDtypeStruct((ATTN_KV_HEADS, T, ATTN_HEAD_DIM), BF16)
    kv_spec = pl.BlockSpec((ATTN_KV_HEADS, QB, ATTN_HEAD_DIM), lambda i: (0, i, 0))
    tab = pl.BlockSpec((QB, 128), lambda i: (i, 0))
    return pl.pallas_call(
        _qkv_rope_body,
        out_shape=(jax.ShapeDtypeStruct((NQB, ATTN_KV_HEADS, ATTN_GROUP, QB, ATTN_HEAD_DIM), BF16), kv_shape, kv_shape),
        grid=(NQB,),
        in_specs=[pl.BlockSpec((QB, ncol), lambda i: (i, 0)), tab, tab],
        out_specs=(pl.BlockSpec((None, ATTN_KV_HEADS, ATTN_GROUP, QB, ATTN_HEAD_DIM), lambda i: (i, 0, 0, 0, 0)),
                   kv_spec, kv_spec),
        compiler_params=_cp("arbitrary"),
        name="qkv_rope")(z, cos, sin)


def _sink_col(sink_ref, kvh, reps):
    cols = [jnp.full((QB, 1), sink_ref[kvh * ATTN_GROUP + g], F32) for g in range(ATTN_GROUP)]
    return jnp.concatenate(cols * reps, axis=0)


def _store_heads(o_ref, o, blk):
    for gp in range(ATTN_GROUP // 2):
        a = o[(blk * ATTN_GROUP + 2 * gp) * QB:(blk * ATTN_GROUP + 2 * gp + 1) * QB]
        b = o[(blk * ATTN_GROUP + 2 * gp + 1) * QB:(blk * ATTN_GROUP + 2 * gp + 2) * QB]
        o_ref[blk * QB:(blk + 1) * QB, gp * 128:(gp + 1) * 128] = jnp.concatenate([a, b], axis=1)


def _attn_ctx_body(sink_ref, q_ref, k_ref, v_ref, o_ref):
    kvh = pl.program_id(1)
    nblk = SEQ // QB
    q = q_ref[...].reshape(nblk * ATTN_GROUP * QB, ATTN_HEAD_DIM)
    s = _dot_nt(q, k_ref[...])
    o = _softmax_pv([s], [v_ref[...]], _sink_col(sink_ref, kvh, nblk)).astype(BF16)
    for blk in range(nblk):
        _store_heads(o_ref, o, blk)


def _attn_lat_body(sink_ref, q_ref, kp_ref, kc_ref, kn_ref, vp_ref, vc_ref, vn_ref, kx_ref, vx_ref, o_ref):
    kvh = pl.program_id(1)
    i = pl.program_id(2)
    q = q_ref[...].reshape(ATTN_GROUP * QB, ATTN_HEAD_DIM)
    k_loc = jnp.concatenate([kp_ref[...], kc_ref[...], kn_ref[...]], axis=0)
    v_loc = jnp.concatenate([vp_ref[...], vc_ref[...], vn_ref[...]], axis=0)
    s_loc = _dot_nt(q, k_loc).reshape(ATTN_GROUP, QB, 3 * QB)
    qpos = i * QB + lax.broadcasted_iota(I32, (QB, 3 * QB), 0)
    kpos = (i - 1) * QB + lax.broadcasted_iota(I32, (QB, 3 * QB), 1)
    ok = (jnp.abs(qpos - kpos) <= ATTN_WINDOW) & (kpos >= 0) & (kpos < DEC_SEQ)
    s_loc = jnp.where(ok[None], s_loc, NEG_INF).reshape(ATTN_GROUP * QB, 3 * QB)
    s_ctx = _dot_nt(q, kx_ref[...].astype(BF16))
    o = _softmax_pv([s_loc, s_ctx], [v_loc, vx_ref[...].astype(BF16)], _sink_col(sink_ref, kvh, 1))
    _store_heads(o_ref, o.astype(BF16), 0)


def _attn_mixer(h, wqkv, sink, cache_k, cache_v):
    z = _matmul(h, wqkv, 0, name="attn_qkv")
    cos, sin = _rope_tables()
    q5, k3, v3 = _qkv_rope(z, cos, sin)
    smem = pl.BlockSpec(memory_space=pltpu.SMEM)
    G, dh, KVH = ATTN_GROUP, ATTN_HEAD_DIM, ATTN_KV_HEADS
    nblk = SEQ // QB
    o_ctx = pl.pallas_call(
        _attn_ctx_body,
        out_shape=jax.ShapeDtypeStruct((TP, NQ_COLS), BF16),
        grid=(BATCH, KVH),
        in_specs=[smem,
                  pl.BlockSpec((nblk, None, G, QB, dh), lambda b, h_: (b, h_, 0, 0, 0)),
                  pl.BlockSpec((None, SEQ, dh), lambda b, h_: (h_, b, 0)),
                  pl.BlockSpec((None, SEQ, dh), lambda b, h_: (h_, b, 0))],
        out_specs=pl.BlockSpec((SEQ, G * dh), lambda b, h_: (b, h_)),
        compiler_params=_cp("arbitrary", "arbitrary"),
        name="attn_ctx")(sink, q5, k3, v3)

    nq = DEC_SEQ // QB
    base = TP // QB

    def kv_spec(off):
        return pl.BlockSpec((None, QB, dh), lambda b, h_, i: (h_, base + b * nq + jnp.clip(i + off, 0, nq - 1), 0))

    cx = pl.BlockSpec((None, None, PAST_LEN, dh), lambda b, h_, i: (b, h_, 0, 0))
    o_lat = pl.pallas_call(
        _attn_lat_body,
        out_shape=jax.ShapeDtypeStruct((TS, NQ_COLS), BF16),
        grid=(DEC_BATCH, KVH, nq),
        in_specs=[smem,
                  pl.BlockSpec((None, None, G, QB, dh), lambda b, h_, i: (base + b * nq + i, h_, 0, 0, 0)),
                  kv_spec(-1), kv_spec(0), kv_spec(1), kv_spec(-1), kv_spec(0), kv_spec(1), cx, cx],
        out_specs=pl.BlockSpec((QB, G * dh), lambda b, h_, i: (b * nq + i, h_)),
        compiler_params=_cp("arbitrary", "arbitrary", "arbitrary"),
        name="attn_lat")(sink, q5, k3, k3, k3, v3, v3, v3,
                         jnp.transpose(cache_k, (0, 2, 1, 3)), jnp.transpose(cache_v, (0, 2, 1, 3)))
    return jnp.concatenate([o_ctx, o_lat], axis=0), z


NA_SCALE = NA_HEAD_DIM ** -0.5
NA_QROWS = 8
NA_KROWS = 16
NA_RB = NA_QROWS * GRID_W
NA_KB = NA_KROWS * GRID_W
NA_R = DEC_SEQ // GRID_W
NA_NBLK = NA_R // NA_QROWS
NA_RPB_W = 2 * NA_COLS - 1
NA_RPB_SIZE = (2 * NA_ROWS - 1) * NA_RPB_W
NA_CTX_HEADS_PER_STEP = 4


def _na_key_row0(a):
    return min(max(NA_QROWS * a - NA_ROWS // 2, 0), NA_R - NA_KROWS)


def _na_bias_body(rpb_ref, o_ref):
    h = pl.program_id(0)
    cq = lax.broadcasted_iota(I32, (GRID_W, 128), 0)
    ck = lax.broadcasted_iota(I32, (GRID_W, 128), 1) % GRID_W
    c0 = jnp.clip(cq - NA_COLS // 2, 0, GRID_W - NA_COLS)
    col_ok = (ck >= c0) & (ck < c0 + NA_COLS)
    dc = jnp.clip(ck - cq + NA_COLS - 1, 0, NA_RPB_W - 1)
    neg = jnp.full((GRID_W, 128), NEG_INF, F32)
    tiles = []
    for dr in range(2 * NA_ROWS - 1):
        t = neg
        for j in range(NA_RPB_W):
            t = jnp.where(dc == j, rpb_ref[h * NA_RPB_SIZE + dr * NA_RPB_W + j], t)
        tiles.append(jnp.where(col_ok, t, NEG_INF))
    left = lax.broadcasted_iota(I32, (GRID_W, 128), 1) < GRID_W
    for var, a in enumerate((0, 1, NA_NBLK - 1)):
        kr0 = _na_key_row0(a)
        for rq in range(NA_QROWS):
            r = NA_QROWS * a + rq
            r0 = min(max(r - NA_ROWS // 2, 0), NA_R - NA_ROWS)
            for p in range(NA_KROWS // 2):
                halves = []
                for kr in (kr0 + 2 * p, kr0 + 2 * p + 1):
                    halves.append(tiles[kr - r + NA_ROWS - 1] if r0 <= kr < r0 + NA_ROWS else neg)
                o_ref[var, rq * GRID_W:(rq + 1) * GRID_W, p * 128:(p + 1) * 128] = (
                    halves[0] if halves[0] is halves[1] else jnp.where(left, halves[0], halves[1]))


def _na_bias(rpb):
    return pl.pallas_call(
        _na_bias_body,
        out_shape=jax.ShapeDtypeStruct((NA_HEADS, 3, NA_RB, NA_KB), F32),
        grid=(NA_HEADS,),
        in_specs=[pl.BlockSpec(memory_space=pltpu.SMEM)],
        out_specs=pl.BlockSpec((None, 3, NA_RB, NA_KB), lambda h_: (h_, 0, 0, 0)),
        compiler_params=_cp("arbitrary"),
        name="na_bias")(rpb.reshape(-1))


def _na_ctx_body(q_ref, k_ref, v_ref, o_ref):
    for hh in range(NA_CTX_HEADS_PER_STEP):
        sl = slice(hh * NA_HEAD_DIM, (hh + 1) * NA_HEAD_DIM)
        s = _dot_nt(q_ref[:, sl].astype(BF16), k_ref[:, sl].astype(BF16)) * NA_SCALE
        o_ref[:, sl] = _softmax_pv([s], [v_ref[:, sl].astype(BF16)]).astype(BF16)


def _na_lat_body(q_ref, k_ref, v_ref, bias_ref, kx_ref, vx_ref, o_ref):
    a = pl.program_id(2)
    k0 = pl.multiple_of(jnp.clip(NA_QROWS * a - NA_ROWS // 2, 0, NA_R - NA_KROWS) * GRID_W, 256)
    q = q_ref[...].astype(BF16)
    s_loc = _dot_nt(q, k_ref[pl.ds(k0, NA_KB), :].astype(BF16)) * NA_SCALE + bias_ref[...]
    s_ctx = _dot_nt(q, kx_ref[...].astype(BF16)) * NA_SCALE
    o = _softmax_pv([s_loc, s_ctx], [v_ref[pl.ds(k0, NA_KB), :].astype(BF16), vx_ref[...].astype(BF16)])
    o_ref[...] = o.astype(BF16)


def _na_mixer(h, wqkv, rpb, cache_k, cache_v):
    z = _matmul(h, wqkv, 0, name="na_qkv")
    nh, dh = NA_HEADS, NA_HEAD_DIM
    hps = NA_CTX_HEADS_PER_STEP
    cw = hps * dh
    o_ctx = pl.pallas_call(
        _na_ctx_body,
        out_shape=jax.ShapeDtypeStruct((TP, nh * dh), BF16),
        grid=(BATCH, nh // hps),
        in_specs=[pl.BlockSpec((SEQ, cw), lambda b, g: (b, g)),
                  pl.BlockSpec((SEQ, cw), lambda b, g: (b, nh // hps + g)),
                  pl.BlockSpec((SEQ, cw), lambda b, g: (b, 2 * (nh // hps) + g))],
        out_specs=pl.BlockSpec((SEQ, cw), lambda b, g: (b, g)),
        compiler_params=_cp("arbitrary", "arbitrary"),
        name="na_ctx")(z, z, z)

    bias = _na_bias(rpb)
    qbase = TP // NA_RB
    sbase = TP // DEC_SEQ

    def var(a):
        return jnp.where(a == 0, 0, jnp.where(a == NA_NBLK - 1, 2, 1))

    cx = pl.BlockSpec((None, PAST_LEN, dh), lambda b, h_, a: (b, 0, h_))
    o_lat = pl.pallas_call(
        _na_lat_body,
        out_shape=jax.ShapeDtypeStruct((TS, nh * dh), BF16),
        grid=(DEC_BATCH, nh, NA_NBLK),
        in_specs=[pl.BlockSpec((NA_RB, dh), lambda b, h_, a: (qbase + b * NA_NBLK + a, h_)),
                  pl.BlockSpec((DEC_SEQ, dh), lambda b, h_, a: (sbase + b, nh + h_)),
                  pl.BlockSpec((DEC_SEQ, dh), lambda b, h_, a: (sbase + b, 2 * nh + h_)),
                  pl.BlockSpec((None, None, NA_RB, NA_KB), lambda b, h_, a: (h_, var(a), 0, 0)),
                  cx, cx],
        out_specs=pl.BlockSpec((NA_RB, dh), lambda b, h_, a: (b * NA_NBLK + a, h_)),
        compiler_params=_cp("arbitrary", "arbitrary", "arbitrary"),
        name="na_lat")(z, z, z, bias, cache_k.reshape(DEC_BATCH, PAST_LEN, nh * dh),
                       cache_v.reshape(DEC_BATCH, PAST_LEN, nh * dh))
    return jnp.concatenate([o_ctx, o_lat], axis=0), z


ML = 256
MLSTM_QSCALE = MLSTM_DK ** -0.5
NQK = MLSTM_HEADS * MLSTM_DK
NV = MLSTM_HEADS * MLSTM_DV


def _log_sigmoid(x):
    return jnp.minimum(x, 0.0) - jnp.log(1.0 + jnp.exp(-jnp.abs(x)))


def _mlstm_chunk(d, q, k, v, grow, gcol, state):
    sgn = 1 - 2 * d
    ti = lax.broadcasted_iota(I32, (ML, ML), 0)
    si = lax.broadcasted_iota(I32, (ML, ML), 1)
    before = (si - ti) * sgn <= 0
    before_t = (ti - si) * sgn <= 0
    i_row, f_row = grow[0:1], _log_sigmoid(grow[1:2])
    i_col, f_col = gcol[:, 0:1], _log_sigmoid(gcol[:, 1:2])
    b_col = jnp.sum(jnp.where(before, f_row, 0.0), axis=1, keepdims=True)
    b_row = jnp.sum(jnp.where(before_t, f_col, 0.0), axis=0, keepdims=True)
    b_last = jnp.sum(f_row, axis=1, keepdims=True)
    r_row = i_row - b_row
    m_prev = state[2] if state is not None else jnp.zeros((1, 1), F32)
    mx_col = jnp.maximum(jnp.max(jnp.where(before, r_row, -jnp.inf), axis=1, keepdims=True), m_prev)
    w = jnp.exp(jnp.where(before, r_row - mx_col, -jnp.inf))
    qs = q * MLSTM_QSCALE
    qb, kb, vb = qs.astype(BF16), k.astype(BF16), v.astype(BF16)
    sqk = _dot_nt(qb, kb) * w
    num = jnp.dot(sqk.astype(BF16), vb, preferred_element_type=F32)
    den = jnp.sum(sqk, axis=1, keepdims=True)
    if state is not None:
        a_col = jnp.exp(m_prev - mx_col)
        num = num + a_col * jnp.dot(qb, state[0].astype(BF16), preferred_element_type=F32)
        den = den + a_col * jnp.sum(qs * state[1], axis=1, keepdims=True)
    h = num * (1.0 / jnp.maximum(jnp.abs(den), jnp.exp(-(b_col + mx_col))))
    g_col = b_last - b_col + i_col
    m_new = jnp.maximum(b_last + m_prev, jnp.max(g_col, axis=0, keepdims=True))
    kw = k * jnp.exp(g_col - m_new)
    c_new = lax.dot_general(kw.astype(BF16), vb, (((0,), (0,)), ((), ())), preferred_element_type=F32)
    n_new = jnp.sum(kw, axis=0, keepdims=True)
    if state is not None:
        decay = jnp.exp(b_last + m_prev - m_new)
        c_new = decay * state[0] + c_new
        n_new = decay * state[1] + n_new
    return h, (c_new, n_new, m_new)


def _mlstm_ctx_body(q_ref, k_ref, v_ref, grow_ref, gcol_ref, h_ref, c_ref, n_ref, m_ref):
    d = pl.program_id(2)
    h, (c, n, m) = _mlstm_chunk(d, q_ref[...], k_ref[...], v_ref[...], grow_ref[...], gcol_ref[...], None)
    h_ref[...] = h
    c_ref[...] = c
    n_ref[...] = n
    m_ref[...] = jnp.broadcast_to(m, m_ref.shape)


def _mlstm_lat_body(m0_ref, q_ref, k_ref, v_ref, grow_ref, gcol_ref, c0_ref, n0_ref, h_ref, c_s, n_s, m_s):
    b, hd, d, c = pl.program_id(0), pl.program_id(1), pl.program_id(2), pl.program_id(3)

    @pl.when(c == 0)
    def _():
        c_s[...] = c0_ref[...]
        n_s[...] = n0_ref[...]
        m_s[...] = jnp.full(m_s.shape, m0_ref[(b * 2 + d) * MLSTM_HEADS + hd], F32)

    state = (c_s[...], n_s[...], m_s[:, 0:1])
    h, (cn, nn, mn) = _mlstm_chunk(d, q_ref[...], k_ref[...], v_ref[...], grow_ref[...], gcol_ref[...], state)
    h_ref[...] = h
    c_s[...] = cn
    n_s[...] = nn
    m_s[...] = jnp.broadcast_to(mn, m_s.shape)


def _mlstm_out_body(hf_ref, hb_ref, o_ref, gn_ref, y_ref):
    for hd in range(MLSTM_HEADS):
        sl = slice(hd * MLSTM_DV, (hd + 1) * MLSTM_DV)
        hs = hf_ref[:, sl] + hb_ref[:, sl]
        hn = hs * lax.rsqrt(jnp.mean(hs * hs, axis=-1, keepdims=True) + LN_EPS) * gn_ref[:, sl]
        y_ref[:, sl] = (jax.nn.sigmoid(o_ref[:, sl]) * hn).astype(BF16)


def _mlstm_mixer(h, win, wgate, bgate, gnorm, state_c, state_n, state_m):
    H, DK, DV = MLSTM_HEADS, MLSTM_DK, MLSTM_DV
    z = _matmul(h, win, 0, name="mlstm_in")
    g = _matmul(h, wgate, 0, bgate, tm=ROW_TILE, name="mlstm_gate")
    g4 = g.reshape(T, 2, 2, H)
    grow = jnp.transpose(g4, (1, 3, 2, 0))
    gcol = jnp.transpose(g4, (1, 3, 0, 2))

    def specs(row_blk):
        return [pl.BlockSpec((ML, DK), lambda *a: (row_blk(*a), a[1])),
                pl.BlockSpec((ML, DK), lambda *a: (row_blk(*a), H + a[1])),
                pl.BlockSpec((ML, DV), lambda *a: (row_blk(*a), H + a[1])),
                pl.BlockSpec((None, None, 2, ML), lambda *a: (a[2], a[1], 0, row_blk(*a))),
                pl.BlockSpec((None, None, ML, 2), lambda *a: (a[2], a[1], row_blk(*a), 0))]

    def hspec(row_blk):
        return pl.BlockSpec((None, ML, DV), lambda *a: (a[2], row_blk(*a), a[1]))

    assert SEQ == ML
    st = lambda shape: pl.BlockSpec((None, None, None) + shape, lambda s, hd, d: (s, d, hd, 0, 0))
    h_ctx, c_new, n_new, m_new = pl.pallas_call(
        _mlstm_ctx_body,
        out_shape=(jax.ShapeDtypeStruct((2, TP, NV), F32),
                   jax.ShapeDtypeStruct((BATCH, 2, H, DK, DV), F32),
                   jax.ShapeDtypeStruct((BATCH, 2, H, 1, DK), F32),
                   jax.ShapeDtypeStruct((BATCH, 2, H, 1, 128), F32)),
        grid=(BATCH, H, 2),
        in_specs=specs(lambda s, hd, d: s),
        out_specs=(hspec(lambda s, hd, d: s), st((DK, DV)), st((1, DK)), st((1, 128))),
        compiler_params=_cp("arbitrary", "arbitrary", "arbitrary"),
        name="mlstm_ctx")(z, z, z, grow, gcol)

    nc = DEC_SEQ // ML
    base = TP // ML

    def lat_blk(b, hd, d, c):
        return base + b * nc + jnp.where(d == 0, c, nc - 1 - c)

    lst = lambda shape: pl.BlockSpec((None, None, None) + shape, lambda b, hd, d, c: (b, d, hd, 0, 0))
    h_lat = pl.pallas_call(
        _mlstm_lat_body,
        out_shape=jax.ShapeDtypeStruct((2, TS, NV), F32),
        grid=(DEC_BATCH, H, 2, nc),
        in_specs=[pl.BlockSpec(memory_space=pltpu.SMEM)] + specs(lat_blk) + [lst((DK, DV)), lst((1, DK))],
        out_specs=pl.BlockSpec((None, ML, DV), lambda b, hd, d, c: (d, lat_blk(b, hd, d, c) - base, hd)),
        scratch_shapes=[pltpu.VMEM((DK, DV), F32), pltpu.VMEM((1, DK), F32), pltpu.VMEM((1, 128), F32)],
        compiler_params=_cp("arbitrary", "arbitrary", "arbitrary", "arbitrary"),
        name="mlstm_lat")(state_m.reshape(-1), z, z, z, grow, gcol, state_c,
                          state_n.reshape(DEC_BATCH, 2, H, 1, DK))

    hh = jnp.concatenate([h_ctx, h_lat], axis=1)
    row = lambda j: pl.BlockSpec((ROW_TILE, NV), lambda i: (i, j))
    y = pl.pallas_call(
        _mlstm_out_body,
        out_shape=jax.ShapeDtypeStruct((T, NV), BF16),
        grid=(T // ROW_TILE,),
        in_specs=[pl.BlockSpec((None, ROW_TILE, NV), lambda i: (0, i, 0)),
                  pl.BlockSpec((None, ROW_TILE, NV), lambda i: (1, i, 0)),
                  row((2 * NQK + NV) // NV), pl.BlockSpec((1, NV), lambda i: (0, 0))],
        out_specs=row(0),
        compiler_params=_cp("arbitrary"),
        name="mlstm_out")(hh, hh, z, gnorm.reshape(1, NV))
    new_state = (c_new, n_new.reshape(BATCH, 2, H, DK), m_new[:, :, :, 0, 0])
    return y, new_state


CONV_HALO = 16
CONV_RT = 64
assert SEQ == ROW_TILE and DEC_SEQ % ROW_TILE == 0


def _conv_body(a_ref, g_ref, ap_ref, gp_ref, an_ref, gn_ref, dw_ref, dwb_ref, lng_ref, lnb_ref, y_ref, u_ref, c_ref):
    i = pl.program_id(0)
    tiles_per_seq = DEC_SEQ // ROW_TILE
    pos = (i - TP // ROW_TILE) % tiles_per_seq
    is_lat = i >= TP // ROW_TILE
    has_prev = is_lat & (pos > 0)
    has_next = is_lat & (pos < tiles_per_seq - 1)

    def glu(a, g):
        return a * jax.nn.sigmoid(g)

    H = CONV_HALO
    u_ref[0:H, :] = jnp.where(has_prev, glu(ap_ref[...], gp_ref[...]), 0.0)
    u_ref[H:H + ROW_TILE, :] = glu(a_ref[...], g_ref[...])
    u_ref[H + ROW_TILE:, :] = jnp.where(has_next, glu(an_ref[...], gn_ref[...]), 0.0)

    off = H - CONV_WIDTH // 2

    def lane_chunk(c, carry):
        c0 = pl.multiple_of(c * 128, 128)
        for r in range(ROW_TILE // CONV_RT):
            acc = jnp.zeros((CONV_RT, 128), F32)
            for j in range(CONV_WIDTH):
                acc = acc + u_ref[pl.ds(r * CONV_RT + off + j, CONV_RT), pl.ds(c0, 128)] * dw_ref[j:j + 1, pl.ds(c0, 128)]
            c_ref[pl.ds(r * CONV_RT, CONV_RT), pl.ds(c0, 128)] = acc
        return carry

    lax.fori_loop(0, D_MODEL // 128, lane_chunk, 0)
    v = c_ref[...] + dwb_ref[...]
    mu = jnp.mean(v, axis=-1, keepdims=True)
    d = v - mu
    var = jnp.mean(d * d, axis=-1, keepdims=True)
    xn = d * lax.rsqrt(var + LN_EPS) * lng_ref[...] + lnb_ref[...]
    y_ref[...] = (xn * jax.nn.sigmoid(xn)).astype(BF16)


def _conv_mixer(h, w1, b1, dw, dwb, lng, lnb):
    z = _matmul(h, w1, 0, b1, name="conv_in")
    D = D_MODEL
    hb = ROW_TILE // CONV_HALO
    nh = T // CONV_HALO
    cur = lambda j: pl.BlockSpec((ROW_TILE, D), lambda i: (i, j))
    prev = lambda j: pl.BlockSpec((CONV_HALO, D), lambda i: (jnp.maximum(i * hb - 1, 0), j))
    nxt = lambda j: pl.BlockSpec((CONV_HALO, D), lambda i: (jnp.minimum((i + 1) * hb, nh - 1), j))
    vec = pl.BlockSpec((None, 1, D), lambda i: (0, 0, 0))
    return pl.pallas_call(
        _conv_body,
        out_shape=jax.ShapeDtypeStruct((T, D), BF16),
        grid=(T // ROW_TILE,),
        in_specs=[cur(0), cur(1), prev(0), prev(1), nxt(0), nxt(1),
                  pl.BlockSpec((None, CONV_WIDTH, D), lambda i: (0, 0, 0)), vec, vec, vec],
        out_specs=cur(0),
        scratch_shapes=[pltpu.VMEM((ROW_TILE + 2 * CONV_HALO, D), F32), pltpu.VMEM((ROW_TILE, D), F32)],
        compiler_params=_cp("arbitrary"),
        name="conv_mid")(z, z, z, z, z, z, dw, dwb.reshape(-1, 1, D), lng.reshape(-1, 1, D), lnb.reshape(-1, 1, D))


MOE_TM = 512
MOE_NT = T * TOP_K // MOE_TM + N_EXPERTS
MOE_TC = 512
MOE_N1 = D_EXPERT // MOE_TC
MOE_TN = 1024
MOE_N2 = D_MODEL // MOE_TN
MOE_CT = 128
TOPK_TILE = 1024


def _topk_body(l_ref, idx_ref, gate_ref):
    l = l_ref[...]
    lane = lax.broadcasted_iota(I32, l.shape, 1)
    vals, idxs = [], []
    for _ in range(TOP_K):
        m = jnp.max(l, axis=-1, keepdims=True)
        ix = jnp.min(jnp.where(l == m, lane, N_EXPERTS), axis=-1, keepdims=True)
        vals.append(m)
        idxs.append(ix)
        l = jnp.where(lane == ix, -jnp.inf, l)
    e = [jnp.exp(v - vals[0]) for v in vals]
    inv = 1.0 / functools.reduce(jnp.add, e)
    kcol = lax.broadcasted_iota(I32, idx_ref.shape, 1)
    idx = jnp.zeros(idx_ref.shape, I32)
    gate = jnp.zeros(gate_ref.shape, F32)
    for k in range(TOP_K):
        idx = jnp.where(kcol == k, idxs[k], idx)
        gate = jnp.where(kcol == k, e[k] * inv, gate)
    idx_ref[...] = idx
    gate_ref[...] = gate


def _topk(logits):
    spec = lambda w: pl.BlockSpec((TOPK_TILE, w), lambda i: (i, 0))
    return pl.pallas_call(
        _topk_body,
        out_shape=(jax.ShapeDtypeStruct((T, TOP_K), I32), jax.ShapeDtypeStruct((T, TOP_K), F32)),
        grid=(T // TOPK_TILE,),
        in_specs=[spec(N_EXPERTS)],
        out_specs=(spec(TOP_K), spec(TOP_K)),
        compiler_params=_cp("arbitrary"),
        name="moe_topk")(logits)


def _moe_plan(top_i):
    A = T * TOP_K
    flat_e = top_i.reshape(A)
    onehot = (flat_e[:, None] == jnp.arange(N_EXPERTS, dtype=I32)[None, :]).astype(I32)
    rank = jnp.take_along_axis(jnp.cumsum(onehot, axis=0), flat_e[:, None], axis=1)[:, 0] - 1
    counts = jnp.sum(onehot, axis=0)
    padded = (counts + MOE_TM - 1) // MOE_TM * MOE_TM
    pad_end = jnp.cumsum(padded)
    dest = (pad_end - padded)[flat_e] + rank
    slot_tok = jnp.zeros((MOE_NT * MOE_TM,), I32).at[dest].set(jnp.arange(A, dtype=I32) // TOP_K)
    tile_start = jnp.arange(MOE_NT, dtype=I32) * MOE_TM
    tile_e = jnp.minimum(jnp.searchsorted(pad_end, tile_start, side='right'), N_EXPERTS - 1).astype(I32)
    tile_on = (tile_start < pad_end[-1]).astype(I32)
    return slot_tok, tile_e, tile_on, dest.astype(I32)


def _gmm_body(te_ref, on_ref, tok_ref, x_hbm, w1g_ref, w1l_ref, b1g_ref, b1l_ref, w2_ref, b2_ref,
              y_ref, xg_ref, xbf_ref, act_ref, sem):
    t, j = pl.program_id(0), pl.program_id(1)
    on = on_ref[t] != 0

    def row_copy(r):
        return pltpu.make_async_copy(x_hbm.at[pl.ds(tok_ref[0, r], 1), :], xg_ref.at[pl.ds(r, 1), :], sem)

    @pl.when(on & (j == 0))
    def _():
        def issue(r, c):
            row_copy(r).start()
            return c

        lax.fori_loop(0, MOE_TM, issue, 0)

        def wait(r, c):
            row_copy(r).wait()
            return c

        lax.fori_loop(0, MOE_TM, wait, 0)
        xbf_ref[...] = xg_ref[...].astype(BF16)

    for jj in range(MOE_N1):
        @pl.when(on & (j == jj))
        def _():
            x = xbf_ref[...]
            glu = jnp.dot(x, w1g_ref[...], preferred_element_type=F32) + b1g_ref[...]
            lin = jnp.dot(x, w1l_ref[...], preferred_element_type=F32) + b1l_ref[...]
            glu = jnp.minimum(glu, SWIGLU_LIMIT)
            lin = jnp.clip(lin, -SWIGLU_LIMIT, SWIGLU_LIMIT)
            act_ref[jj] = (glu * jax.nn.sigmoid(SWIGLU_ALPHA * glu) * (lin + 1.0)).astype(BF16)

    @pl.when(on & (j >= MOE_N1))
    def _():
        acc = jnp.broadcast_to(b2_ref[...], y_ref.shape)
        for jj in range(MOE_N1):
            w2 = w2_ref[jj * MOE_TC:(jj + 1) * MOE_TC, :].astype(BF16)
            acc = acc + jnp.dot(act_ref[jj], w2, preferred_element_type=F32)
        y_ref[...] = acc

    @pl.when(jnp.logical_not(on) & (j >= MOE_N1))
    def _():
        y_ref[...] = jnp.zeros(y_ref.shape, F32)


def _gmm(x, slot_tok, tile_e, tile_on, w1p, b1p, w2, b2, lidx):
    D, DE = D_MODEL, D_EXPERT
    c1 = lambda j: jnp.minimum(j, MOE_N1 - 1)
    c2 = lambda j: jnp.maximum(j - MOE_N1, 0)
    grid_spec = pltpu.PrefetchScalarGridSpec(
        num_scalar_prefetch=2,
        grid=(MOE_NT, MOE_N1 + MOE_N2),
        in_specs=[pl.BlockSpec((None, 1, MOE_TM), lambda t, j, te, on: (t, 0, 0), memory_space=pltpu.SMEM),
                  pl.BlockSpec(memory_space=pl.ANY),
                  pl.BlockSpec((None, None, D, MOE_TC), lambda t, j, te, on: (lidx, te[t], 0, c1(j))),
                  pl.BlockSpec((None, None, D, MOE_TC), lambda t, j, te, on: (lidx, te[t], 0, MOE_N1 + c1(j))),
                  pl.BlockSpec((None, None, 1, MOE_TC), lambda t, j, te, on: (lidx, te[t], 0, c1(j))),
                  pl.BlockSpec((None, None, 1, MOE_TC), lambda t, j, te, on: (lidx, te[t], 0, MOE_N1 + c1(j))),
                  pl.BlockSpec((None, None, DE, MOE_TN), lambda t, j, te, on: (lidx, te[t], 0, c2(j))),
                  pl.BlockSpec((None, None, 1, MOE_TN), lambda t, j, te, on: (lidx, te[t], 0, c2(j)))],
        out_specs=pl.BlockSpec((MOE_TM, MOE_TN), lambda t, j, te, on: (t, c2(j))),
        scratch_shapes=[pltpu.VMEM((MOE_TM, D), F32), pltpu.VMEM((MOE_TM, D), BF16),
                        pltpu.VMEM((MOE_N1, MOE_TM, MOE_TC), BF16), pltpu.SemaphoreType.DMA(())])
    return pl.pallas_call(
        _gmm_body,
        out_shape=jax.ShapeDtypeStruct((MOE_NT * MOE_TM, D), F32),
        grid_spec=grid_spec,
        compiler_params=_cp("arbitrary", "arbitrary"),
        name="moe_gmm")(tile_e, tile_on, slot_tok.reshape(MOE_NT, 1, MOE_TM), x, w1p, w1p, b1p, b1p,
                        w2, b2.reshape(b2.shape[0], N_EXPERTS, 1, D))


def _combine_body(slot_ref, gate_ref, y_hbm, o_ref, buf_ref, sem):
    def row_copy(r, k):
        return pltpu.make_async_copy(y_hbm.at[pl.ds(slot_ref[0, r * TOP_K + k], 1), :],
                                     buf_ref.at[k, pl.ds(r, 1), :], sem)

    def issue(r, c):
        for k in range(TOP_K):
            row_copy(r, k).start()
        return c

    lax.fori_loop(0, MOE_CT, issue, 0)

    def wait(r, c):
        for k in range(TOP_K):
            row_copy(r, k).wait()
        return c

    lax.fori_loop(0, MOE_CT, wait, 0)
    g = gate_ref[...]
    acc = buf_ref[0] * g[:, 0:1]
    for k in range(1, TOP_K):
        acc = acc + buf_ref[k] * g[:, k:k + 1]
    o_ref[...] = acc


def _combine(yb, slot_of, gates):
    n = T // MOE_CT
    return pl.pallas_call(
        _combine_body,
        out_shape=jax.ShapeDtypeStruct((T, D_MODEL), F32),
        grid=(n,),
        in_specs=[pl.BlockSpec((None, 1, MOE_CT * TOP_K), lambda i: (i, 0, 0), memory_space=pltpu.SMEM),
                  pl.BlockSpec((MOE_CT, TOP_K), lambda i: (i, 0)),
                  pl.BlockSpec(memory_space=pl.ANY)],
        out_specs=pl.BlockSpec((MOE_CT, D_MODEL), lambda i: (i, 0)),
        scratch_shapes=[pltpu.VMEM((TOP_K, MOE_CT, D_MODEL), F32), pltpu.SemaphoreType.DMA(())],
        compiler_params=_cp("arbitrary"),
        name="moe_combine")(slot_of.reshape(n, 1, MOE_CT * TOP_K), gates, yb)


def _moe(h, wr, br, w1p, b1p, w2, b2, lidx):
    logits = _matmul(h, wr, lidx, br, tm=ROW_TILE, name="moe_router")
    top_i, gates = _topk(logits)
    slot_tok, tile_e, tile_on, slot_of = _moe_plan(top_i)
    yb = _gmm(h, slot_tok, tile_e, tile_on, w1p, b1p, w2, b2, lidx)
    return _combine(yb, slot_of, gates)


def kernel(x_prompt, x_sample, cache_attn_k, cache_attn_v, cache_na_k, cache_na_v, state_mlstm_C, state_mlstm_n, state_mlstm_m, c, c_ctx, ada_w, ada_b, ln1_g, ln1_b, ln2_g, ln2_b, attn_wqkv, attn_wo, attn_sink, na_wqkv, na_wo, na_rpb, mlstm_win, mlstm_wgate, mlstm_bgate, mlstm_gnorm, mlstm_wo, conv_w1, conv_b1, conv_dw, conv_dwb, conv_ln_g, conv_ln_b, conv_w2, conv_b2, moe_wr, moe_br, moe_w1, moe_b1, moe_w2, moe_b2):
    D = D_MODEL
    x = jnp.concatenate([x_prompt.reshape(TP, D), x_sample.reshape(TS, D)], axis=0)
    cond = jnp.concatenate([c_ctx[None, :], c, jnp.zeros((ADA_ROWS - 1 - DEC_BATCH, D), F32)], axis=0)
    mods = _ada(cond, ada_w, ada_b)[:, :N_SEG].reshape(DEPTH, N_SEG, 6, D)
    w1p = jnp.concatenate([moe_w1[..., 0::2], moe_w1[..., 1::2]], axis=-1).astype(BF16)
    b1p = jnp.concatenate([moe_b1[..., 0::2], moe_b1[..., 1::2]], axis=-1).reshape(DEPTH, N_EXPERTS, 1, 2 * D_EXPERT)

    h = _modulate(x, mods[0])
    outs = {}
    for i in range(DEPTH):
        mixer = i % 4
        if mixer == 0:
            o, z = _attn_mixer(h, attn_wqkv, attn_sink[0], cache_attn_k[:, 0], cache_attn_v[:, 0])
            kv = z[:TP, NQ_COLS:].reshape(BATCH, 1, SEQ, 2, ATTN_KV_HEADS, ATTN_HEAD_DIM)
            outs['ak'], outs['av'] = kv[:, :, :, 0], kv[:, :, :, 1]
            y = _matmul(o, attn_wo, 0, name="attn_out")
        elif mixer == 1:
            o, z = _na_mixer(h, na_wqkv, na_rpb[0], cache_na_k[:, 0], cache_na_v[:, 0])
            kv = z[:TP, NA_HEADS * NA_HEAD_DIM:].reshape(BATCH, 1, SEQ, 2, NA_HEADS, NA_HEAD_DIM)
            outs['nk'], outs['nv'] = kv[:, :, :, 0], kv[:, :, :, 1]
            y = _matmul(o, na_wo, 0, name="na_out")
        elif mixer == 2:
            o, st = _mlstm_mixer(h, mlstm_win, mlstm_wgate, mlstm_bgate, mlstm_gnorm[0],
                                 state_mlstm_C[:, 0], state_mlstm_n[:, 0], state_mlstm_m[:, 0])
            outs['C'], outs['n'], outs['m'] = (s[:, None] for s in st)
            y = _matmul(o, mlstm_wo, 0, name="mlstm_out_proj")
        else:
            o = _conv_mixer(h, conv_w1, conv_b1, conv_dw[0:1], conv_dwb[0:1], conv_ln_g[0:1], conv_ln_b[0:1])
            y = _matmul(o, conv_w2, 0, conv_b2, name="conv_out")
        x, h2 = _ln_mod(x, y, mods[i], ln1_g, ln1_b, i, 2, mods[i], 3, h_dtype=F32)
        f = _moe(h2, moe_wr, moe_br, w1p, b1p, moe_w2, moe_b2, i)
        if i + 1 < DEPTH:
            x, h = _ln_mod(x, f, mods[i], ln2_g, ln2_b, i, 5, mods[i + 1], 0)
        else:
            x = _ln_mod(x, f, mods[i], ln2_g, ln2_b, i, 5)
    return (x[:TP].reshape(BATCH, SEQ, D), x[TP:].reshape(DEC_BATCH, DEC_SEQ, D),
            outs['ak'], outs['av'], outs['nk'], outs['nv'], outs['C'], outs['n'], outs['m'])
```

```python
import functools

import jax
import jax.numpy as jnp
from jax import lax
from jax.experimental import pallas as pl
from jax.experimental.pallas import tpu as pltpu

F32 = jnp.float32
BF16 = jnp.bfloat16
I32 = jnp.int32

D_MODEL = 4096
BATCH = 16
SEQ = 256
DEPTH = 4
DEC_BATCH = 2
DEC_SEQ = 4096
PAST_LEN = 256
GRID_W = 64
DN_ALPHA = (2 * DEPTH) ** 0.25
LN_EPS = 1e-5
NEG_INF = -1e30

ATTN_HEADS = 64
ATTN_KV_HEADS = 8
ATTN_HEAD_DIM = 64
ATTN_GROUP = ATTN_HEADS // ATTN_KV_HEADS
ATTN_WINDOW = 128
ROPE_BASE = 10000.0

NA_HEADS = 32
NA_HEAD_DIM = 128
NA_ROWS = 8
NA_COLS = 16

MLSTM_HEADS = 8
MLSTM_DK = 256
MLSTM_DV = 512

CONV_WIDTH = 31

N_EXPERTS = 32
TOP_K = 4
D_EXPERT = 1024
SWIGLU_LIMIT = 7.0
SWIGLU_ALPHA = 1.702

TP = BATCH * SEQ
TS = DEC_BATCH * DEC_SEQ
T = TP + TS
SEG = DEC_SEQ
N_SEG = T // SEG
assert TP == SEG and T % SEG == 0

V7X_VMEM_LIMIT_BYTES = 56 * 2**20


def _cp(*sem):
    return pltpu.CompilerParams(dimension_semantics=sem, vmem_limit_bytes=V7X_VMEM_LIMIT_BYTES)


def _mm_body(x_ref, w_ref, b_ref, o_ref, wbf_ref):
    @pl.when(pl.program_id(1) == 0)
    def _():
        wbf_ref[...] = w_ref[...].astype(BF16)

    acc = jnp.dot(x_ref[...].astype(BF16), wbf_ref[...], preferred_element_type=F32)
    if b_ref is not None:
        acc = acc + b_ref[...]
    o_ref[...] = acc.astype(o_ref.dtype)


def _mm_body_nobias(x_ref, w_ref, o_ref, wbf_ref):
    _mm_body(x_ref, w_ref, None, o_ref, wbf_ref)


def _matmul(x, w, widx, b=None, *, out_dtype=F32, tm=1024, tn=512, name):
    M, K = x.shape
    N = w.shape[2]
    tm, tn = min(tm, M), min(tn, N)
    assert M % tm == 0 and N % tn == 0
    in_specs = [pl.BlockSpec((tm, K), lambda j, i: (i, 0)),
                pl.BlockSpec((None, K, tn), lambda j, i: (widx, 0, j))]
    args = [x, w]
    if b is not None:
        in_specs.append(pl.BlockSpec((None, 1, tn), lambda j, i: (widx, 0, j)))
        args.append(b.reshape(b.shape[0], 1, N))
    return pl.pallas_call(
        _mm_body if b is not None else _mm_body_nobias,
        out_shape=jax.ShapeDtypeStruct((M, N), out_dtype),
        grid=(N // tn, M // tm),
        in_specs=in_specs,
        out_specs=pl.BlockSpec((tm, tn), lambda j, i: (i, j)),
        scratch_shapes=[pltpu.VMEM((K, tn), BF16)],
        compiler_params=_cp("arbitrary", "arbitrary"),
        name=name)(*args)


ADA_ROWS = 16
ADA_TN = 512


def _ada_body(c_ref, w_ref, b_ref, o_ref):
    c = c_ref[...]
    s = (c * jax.nn.sigmoid(c)).astype(BF16)
    o_ref[...] = jnp.dot(s, w_ref[...].astype(BF16), preferred_element_type=F32) + b_ref[...]


def _ada(cond, ada_w, ada_b):
    L, D, N = ada_w.shape
    return pl.pallas_call(
        _ada_body,
        out_shape=jax.ShapeDtypeStruct((L, ADA_ROWS, N), F32),
        grid=(L, N // ADA_TN),
        in_specs=[pl.BlockSpec((ADA_ROWS, D), lambda l, j: (0, 0)),
                  pl.BlockSpec((None, D, ADA_TN), lambda l, j: (l, 0, j)),
                  pl.BlockSpec((None, 1, ADA_TN), lambda l, j: (l, 0, j))],
        out_specs=pl.BlockSpec((None, ADA_ROWS, ADA_TN), lambda l, j: (l, 0, j)),
        compiler_params=_cp("arbitrary", "arbitrary"),
        name="ada")(cond, ada_w, ada_b.reshape(L, 1, N))


ROW_TILE = 256


def _seg_spec():
    return pl.BlockSpec((None, 6, D_MODEL), lambda i: (i // (SEG // ROW_TILE), 0, 0))


def _modulate_body(x_ref, m_ref, h_ref):
    m = m_ref[...]
    h_ref[...] = (x_ref[...] * (1.0 + m[1:2]) + m[0:1]).astype(BF16)


def _modulate(x, mods):
    return pl.pallas_call(
        _modulate_body,
        out_shape=jax.ShapeDtypeStruct((T, D_MODEL), BF16),
        grid=(T // ROW_TILE,),
        in_specs=[pl.BlockSpec((ROW_TILE, D_MODEL), lambda i: (i, 0)), _seg_spec()],
        out_specs=pl.BlockSpec((ROW_TILE, D_MODEL), lambda i: (i, 0)),
        compiler_params=_cp("arbitrary"),
        name="modulate")(x, mods)


def _ln_mod_body(x_ref, y_ref, m_ref, g_ref, b_ref, mn_ref, xo_ref, ho_ref, *, gate_row, shift_row, h_dtype):
    m = m_ref[...]
    v = DN_ALPHA * x_ref[...] + (1.0 + m[gate_row:gate_row + 1]) * y_ref[...]
    mu = jnp.mean(v, axis=-1, keepdims=True)
    d = v - mu
    var = jnp.mean(d * d, axis=-1, keepdims=True)
    xn = d * lax.rsqrt(var + LN_EPS) * g_ref[...] + b_ref[...]
    xo_ref[...] = xn
    if ho_ref is not None:
        mn = mn_ref[...]
        ho_ref[...] = (xn * (1.0 + mn[shift_row + 1:shift_row + 2]) + mn[shift_row:shift_row + 1]).astype(h_dtype)


def _ln_body_last(x_ref, y_ref, m_ref, g_ref, b_ref, xo_ref, *, gate_row):
    _ln_mod_body(x_ref, y_ref, m_ref, g_ref, b_ref, None, xo_ref, None, gate_row=gate_row, shift_row=0, h_dtype=None)


def _ln_mod(x, y, mods, g, b, lidx, gate_row, mods_next=None, shift_row=0, h_dtype=BF16):
    row = pl.BlockSpec((ROW_TILE, D_MODEL), lambda i: (i, 0))
    vec = pl.BlockSpec((None, 1, D_MODEL), lambda i: (lidx, 0, 0))
    in_specs = [row, row, _seg_spec(), vec, vec]
    args = [x, y, mods, g.reshape(-1, 1, D_MODEL), b.reshape(-1, 1, D_MODEL)]
    if mods_next is None:
        body = functools.partial(_ln_body_last, gate_row=gate_row)
        out_shape = jax.ShapeDtypeStruct((T, D_MODEL), F32)
        out_specs = row
    else:
        body = functools.partial(_ln_mod_body, gate_row=gate_row, shift_row=shift_row, h_dtype=h_dtype)
        in_specs.append(_seg_spec())
        args.append(mods_next)
        out_shape = (jax.ShapeDtypeStruct((T, D_MODEL), F32), jax.ShapeDtypeStruct((T, D_MODEL), h_dtype))
        out_specs = (row, row)
    return pl.pallas_call(body, out_shape=out_shape, grid=(T // ROW_TILE,), in_specs=in_specs,
                          out_specs=out_specs, compiler_params=_cp("arbitrary"), name="ln_mod")(*args)


def _dot_nt(a, b):
    return lax.dot_general(a, b, (((1,), (1,)), ((), ())), preferred_element_type=F32)


def _softmax_pv(scores, values, sink_col=None):
    m = functools.reduce(jnp.maximum, [jnp.max(s, axis=-1, keepdims=True) for s in scores])
    if sink_col is not None:
        m = jnp.maximum(m, sink_col)
    den = jnp.exp(sink_col - m) if sink_col is not None else 0.0
    out = None
    for s, v in zip(scores, values):
        e = jnp.exp(s - m)
        den = den + jnp.sum(e, axis=-1, keepdims=True)
        pv = jnp.dot(e.astype(BF16), v, preferred_element_type=F32)
        out = pv if out is None else out + pv
    return out * (1.0 / den)


QB = 128
NQB = T // QB
ATTN_SCALE = ATTN_HEAD_DIM ** -0.5
NQ_COLS = ATTN_HEADS * ATTN_HEAD_DIM
NKV_COLS = ATTN_KV_HEADS * ATTN_HEAD_DIM


def _rope_tables():
    lane = jnp.arange(128)
    hd = lane % ATTN_HEAD_DIM
    n = ATTN_HEAD_DIM // 4
    inv = ROPE_BASE ** (-(hd % n).astype(F32) / n)
    t = jnp.arange(DEC_SEQ)
    row = (t // GRID_W).astype(F32)
    col = (t % GRID_W).astype(F32)
    pos = jnp.where((hd < ATTN_HEAD_DIM // 2)[None, :], row[:, None], col[:, None])
    ang = pos * inv[None, :]
    sign = jnp.where((hd % (2 * n)) < n, -1.0, 1.0).astype(F32)
    cos = jnp.concatenate([jnp.ones((TP, 128), F32)] + [jnp.cos(ang)] * DEC_BATCH)
    sin = jnp.concatenate([jnp.zeros((TP, 128), F32)] + [jnp.sin(ang) * sign[None, :]] * DEC_BATCH)
    return cos, sin


def _qkv_rope_body(z_ref, cos_ref, sin_ref, q_ref, k_ref, v_ref):
    cos = cos_ref[...]
    sin = sin_ref[...]
    lane = lax.broadcasted_iota(I32, (QB, 128), 1)
    first = (lane % 32) < 16

    def rope(x):
        xr = jnp.where(first, pltpu.roll(x, 112, 1), pltpu.roll(x, 16, 1))
        return x * cos + xr * sin

    for c in range(NQ_COLS // 128):
        y = (rope(z_ref[:, c * 128:(c + 1) * 128]) * ATTN_SCALE).astype(BF16)
        kvh, g = (2 * c) // ATTN_GROUP, (2 * c) % ATTN_GROUP
        q_ref[kvh, g] = y[:, :64]
        q_ref[kvh, g + 1] = y[:, 64:]
    for c in range(NKV_COLS // 128):
        c0 = NQ_COLS + c * 128
        y = rope(z_ref[:, c0:c0 + 128]).astype(BF16)
        k_ref[2 * c] = y[:, :64]
        k_ref[2 * c + 1] = y[:, 64:]
        c0 = NQ_COLS + NKV_COLS + c * 128
        y = z_ref[:, c0:c0 + 128].astype(BF16)
        v_ref[2 * c] = y[:, :64]
        v_ref[2 * c + 1] = y[:, 64:]


def _qkv_rope(z, cos, sin):
    ncol = z.shape[1]
    kv_shape = jax.ShapeDtypeStruct((ATTN_KV_HEADS, T, ATTN_HEAD_DIM), BF16)
    kv_spec = pl.BlockSpec((ATTN_KV_HEADS, QB, ATTN_HEAD_DIM), lambda i: (0, i, 0))
    tab = pl.BlockSpec((QB, 128), lambda i: (i, 0))
    return pl.pallas_call(
        _qkv_rope_body,
        out_shape=(jax.ShapeDtypeStruct((NQB, ATTN_KV_HEADS, ATTN_GROUP, QB, ATTN_HEAD_DIM), BF16), kv_shape, kv_shape),
        grid=(NQB,),
        in_specs=[pl.BlockSpec((QB, ncol), lambda i: (i, 0)), tab, tab],
        out_specs=(pl.BlockSpec((None, ATTN_KV_HEADS, ATTN_GROUP, QB, ATTN_HEAD_DIM), lambda i: (i, 0, 0, 0, 0)),
                   kv_spec, kv_spec),
        compiler_params=_cp("arbitrary"),
        name="qkv_rope")(z, cos, sin)


def _sink_col(sink_ref, kvh, reps):
    cols = [jnp.full((QB, 1), sink_ref[kvh * ATTN_GROUP + g], F32) for g in range(ATTN_GROUP)]
    return jnp.concatenate(cols * reps, axis=0)


def _store_heads(o_ref, o, blk):
    for gp in range(ATTN_GROUP // 2):
        a = o[(blk * ATTN_GROUP + 2 * gp) * QB:(blk * ATTN_GROUP + 2 * gp + 1) * QB]
        b = o[(blk * ATTN_GROUP + 2 * gp + 1) * QB:(blk * ATTN_GROUP + 2 * gp + 2) * QB]
        o_ref[blk * QB:(blk + 1) * QB, gp * 128:(gp + 1) * 128] = jnp.concatenate([a, b], axis=1)


def _attn_ctx_body(sink_ref, q_ref, k_ref, v_ref, o_ref):
    kvh = pl.program_id(1)
    nblk = SEQ // QB
    q = q_ref[...].reshape(nblk * ATTN_GROUP * QB, ATTN_HEAD_DIM)
    s = _dot_nt(q, k_ref[...])
    o = _softmax_pv([s], [v_ref[...]], _sink_col(sink_ref, kvh, nblk)).astype(BF16)
    for blk in range(nblk):
        _store_heads(o_ref, o, blk)


def _attn_lat_body(sink_ref, q_ref, kp_ref, kc_ref, kn_ref, vp_ref, vc_ref, vn_ref, kx_ref, vx_ref, o_ref):
    kvh = pl.program_id(1)
    i = pl.program_id(2)
    q = q_ref[...].reshape(ATTN_GROUP * QB, ATTN_HEAD_DIM)
    k_loc = jnp.concatenate([kp_ref[...], kc_ref[...], kn_ref[...]], axis=0)
    v_loc = jnp.concatenate([vp_ref[...], vc_ref[...], vn_ref[...]], axis=0)
    s_loc = _dot_nt(q, k_loc).reshape(ATTN_GROUP, QB, 3 * QB)
    qpos = i * QB + lax.broadcasted_iota(I32, (QB, 3 * QB), 0)
    kpos = (i - 1) * QB + lax.broadcasted_iota(I32, (QB, 3 * QB), 1)
    ok = (jnp.abs(qpos - kpos) <= ATTN_WINDOW) & (kpos >= 0) & (kpos < DEC_SEQ)
    s_loc = jnp.where(ok[None], s_loc, NEG_INF).reshape(ATTN_GROUP * QB, 3 * QB)
    s_ctx = _dot_nt(q, kx_ref[...].astype(BF16))
    o = _softmax_pv([s_loc, s_ctx], [v_loc, vx_ref[...].astype(BF16)], _sink_col(sink_ref, kvh, 1))
    _store_heads(o_ref, o.astype(BF16), 0)


def _attn_mixer(h, wqkv, sink, cache_k, cache_v):
    z = _matmul(h, wqkv, 0, name="attn_qkv")
    cos, sin = _rope_tables()
    q5, k3, v3 = _qkv_rope(z, cos, sin)
    smem = pl.BlockSpec(memory_space=pltpu.SMEM)
    G, dh, KVH = ATTN_GROUP, ATTN_HEAD_DIM, ATTN_KV_HEADS
    nblk = SEQ // QB
    o_ctx = pl.pallas_call(
        _attn_ctx_body,
        out_shape=jax.ShapeDtypeStruct((TP, NQ_COLS), BF16),
        grid=(BATCH, KVH),
        in_specs=[smem,
                  pl.BlockSpec((nblk, None, G, QB, dh), lambda b, h_: (b, h_, 0, 0, 0)),
                  pl.BlockSpec((None, SEQ, dh), lambda b, h_: (h_, b, 0)),
                  pl.BlockSpec((None, SEQ, dh), lambda b, h_: (h_, b, 0))],
        out_specs=pl.BlockSpec((SEQ, G * dh), lambda b, h_: (b, h_)),
        compiler_params=_cp("arbitrary", "arbitrary"),
        name="attn_ctx")(sink, q5, k3, v3)

    nq = DEC_SEQ // QB
    base = TP // QB

    def kv_spec(off):
        return pl.BlockSpec((None, QB, dh), lambda b, h_, i: (h_, base + b * nq + jnp.clip(i + off, 0, nq - 1), 0))

    cx = pl.BlockSpec((None, None, PAST_LEN, dh), lambda b, h_, i: (b, h_, 0, 0))
    o_lat = pl.pallas_call(
        _attn_lat_body,
        out_shape=jax.ShapeDtypeStruct((TS, NQ_COLS), BF16),
        grid=(DEC_BATCH, KVH, nq),
        in_specs=[smem,
                  pl.BlockSpec((None, None, G, QB, dh), lambda b, h_, i: (base + b * nq + i, h_, 0, 0, 0)),
                  kv_spec(-1), kv_spec(0), kv_spec(1), kv_spec(-1), kv_spec(0), kv_spec(1), cx, cx],
        out_specs=pl.BlockSpec((QB, G * dh), lambda b, h_, i: (b * nq + i, h_)),
        compiler_params=_cp("arbitrary", "arbitrary", "arbitrary"),
        name="attn_lat")(sink, q5, k3, k3, k3, v3, v3, v3,
                         jnp.transpose(cache_k, (0, 2, 1, 3)), jnp.transpose(cache_v, (0, 2, 1, 3)))
    return jnp.concatenate([o_ctx, o_lat], axis=0), z


NA_SCALE = NA_HEAD_DIM ** -0.5
NA_QROWS = 8
NA_KROWS = 16
NA_RB = NA_QROWS * GRID_W
NA_KB = NA_KROWS * GRID_W
NA_R = DEC_SEQ // GRID_W
NA_NBLK = NA_R // NA_QROWS
NA_RPB_W = 2 * NA_COLS - 1
NA_RPB_SIZE = (2 * NA_ROWS - 1) * NA_RPB_W
NA_CTX_HEADS_PER_STEP = 4


def _na_key_row0(a):
    return min(max(NA_QROWS * a - NA_ROWS // 2, 0), NA_R - NA_KROWS)


def _na_bias_body(rpb_ref, o_ref):
    h = pl.program_id(0)
    cq = lax.broadcasted_iota(I32, (GRID_W, 128), 0)
    ck = lax.broadcasted_iota(I32, (GRID_W, 128), 1) % GRID_W
    c0 = jnp.clip(cq - NA_COLS // 2, 0, GRID_W - NA_COLS)
    col_ok = (ck >= c0) & (ck < c0 + NA_COLS)
    dc = jnp.clip(ck - cq + NA_COLS - 1, 0, NA_RPB_W - 1)
    neg = jnp.full((GRID_W, 128), NEG_INF, F32)
    tiles = []
    for dr in range(2 * NA_ROWS - 1):
        t = neg
        for j in range(NA_RPB_W):
            t = jnp.where(dc == j, rpb_ref[h * NA_RPB_SIZE + dr * NA_RPB_W + j], t)
        tiles.append(jnp.where(col_ok, t, NEG_INF))
    left = lax.broadcasted_iota(I32, (GRID_W, 128), 1) < GRID_W
    for var, a in enumerate((0, 1, NA_NBLK - 1)):
        kr0 = _na_key_row0(a)
        for rq in range(NA_QROWS):
            r = NA_QROWS * a + rq
            r0 = min(max(r - NA_ROWS // 2, 0), NA_R - NA_ROWS)
            for p in range(NA_KROWS // 2):
                halves = []
                for kr in (kr0 + 2 * p, kr0 + 2 * p + 1):
                    halves.append(tiles[kr - r + NA_ROWS - 1] if r0 <= kr < r0 + NA_ROWS else neg)
                o_ref[var, rq * GRID_W:(rq + 1) * GRID_W, p * 128:(p + 1) * 128] = (
                    halves[0] if halves[0] is halves[1] else jnp.where(left, halves[0], halves[1]))


def _na_bias(rpb):
    return pl.pallas_call(
        _na_bias_body,
        out_shape=jax.ShapeDtypeStruct((NA_HEADS, 3, NA_RB, NA_KB), F32),
        grid=(NA_HEADS,),
        in_specs=[pl.BlockSpec(memory_space=pltpu.SMEM)],
        out_specs=pl.BlockSpec((None, 3, NA_RB, NA_KB), lambda h_: (h_, 0, 0, 0)),
        compiler_params=_cp("arbitrary"),
        name="na_bias")(rpb.reshape(-1))


def _na_ctx_body(q_ref, k_ref, v_ref, o_ref):
    for hh in range(NA_CTX_HEADS_PER_STEP):
        sl = slice(hh * NA_HEAD_DIM, (hh + 1) * NA_HEAD_DIM)
        s = _dot_nt(q_ref[:, sl].astype(BF16), k_ref[:, sl].astype(BF16)) * NA_SCALE
        o_ref[:, sl] = _softmax_pv([s], [v_ref[:, sl].astype(BF16)]).astype(BF16)


def _na_lat_body(q_ref, k_ref, v_ref, bias_ref, kx_ref, vx_ref, o_ref):
    a = pl.program_id(2)
    k0 = pl.multiple_of(jnp.clip(NA_QROWS * a - NA_ROWS // 2, 0, NA_R - NA_KROWS) * GRID_W, 256)
    q = q_ref[...].astype(BF16)
    s_loc = _dot_nt(q, k_ref[pl.ds(k0, NA_KB), :].astype(BF16)) * NA_SCALE + bias_ref[...]
    s_ctx = _dot_nt(q, kx_ref[...].astype(BF16)) * NA_SCALE
    o = _softmax_pv([s_loc, s_ctx], [v_ref[pl.ds(k0, NA_KB), :].astype(BF16), vx_ref[...].astype(BF16)])
    o_ref[...] = o.astype(BF16)


def _na_mixer(h, wqkv, rpb, cache_k, cache_v):
    z = _matmul(h, wqkv, 0, name="na_qkv")
    nh, dh = NA_HEADS, NA_HEAD_DIM
    hps = NA_CTX_HEADS_PER_STEP
    cw = hps * dh
    o_ctx = pl.pallas_call(
        _na_ctx_body,
        out_shape=jax.ShapeDtypeStruct((TP, nh * dh), BF16),
        grid=(BATCH, nh // hps),
        in_specs=[pl.BlockSpec((SEQ, cw), lambda b, g: (b, g)),
                  pl.BlockSpec((SEQ, cw), lambda b, g: (b, nh // hps + g)),
                  pl.BlockSpec((SEQ, cw), lambda b, g: (b, 2 * (nh // hps) + g))],
        out_specs=pl.BlockSpec((SEQ, cw), lambda b, g: (b, g)),
        compiler_params=_cp("arbitrary", "arbitrary"),
        name="na_ctx")(z, z, z)

    bias = _na_bias(rpb)
    qbase = TP // NA_RB
    sbase = TP // DEC_SEQ

    def var(a):
        return jnp.where(a == 0, 0, jnp.where(a == NA_NBLK - 1, 2, 1))

    cx = pl.BlockSpec((None, PAST_LEN, dh), lambda b, h_, a: (b, 0, h_))
    o_lat = pl.pallas_call(
        _na_lat_body,
        out_shape=jax.ShapeDtypeStruct((TS, nh * dh), BF16),
        grid=(DEC_BATCH, nh, NA_NBLK),
        in_specs=[pl.BlockSpec((NA_RB, dh), lambda b, h_, a: (qbase + b * NA_NBLK + a, h_)),
                  pl.BlockSpec((DEC_SEQ, dh), lambda b, h_, a: (sbase + b, nh + h_)),
                  pl.BlockSpec((DEC_SEQ, dh), lambda b, h_, a: (sbase + b, 2 * nh + h_)),
                  pl.BlockSpec((None, None, NA_RB, NA_KB), lambda b, h_, a: (h_, var(a), 0, 0)),
                  cx, cx],
        out_specs=pl.BlockSpec((NA_RB, dh), lambda b, h_, a: (b * NA_NBLK + a, h_)),
        compiler_params=_cp("arbitrary", "arbitrary", "arbitrary"),
        name="na_lat")(z, z, z, bias, cache_k.reshape(DEC_BATCH, PAST_LEN, nh * dh),
                       cache_v.reshape(DEC_BATCH, PAST_LEN, nh * dh))
    return jnp.concatenate([o_ctx, o_lat], axis=0), z


ML = 256
MLSTM_QSCALE = MLSTM_DK ** -0.5
NQK = MLSTM_HEADS * MLSTM_DK
NV = MLSTM_HEADS * MLSTM_DV


def _log_sigmoid(x):
    return jnp.minimum(x, 0.0) - jnp.log(1.0 + jnp.exp(-jnp.abs(x)))


def _mlstm_chunk(d, q, k, v, grow, gcol, state):
    sgn = 1 - 2 * d
    ti = lax.broadcasted_iota(I32, (ML, ML), 0)
    si = lax.broadcasted_iota(I32, (ML, ML), 1)
    before = (si - ti) * sgn <= 0
    before_t = (ti - si) * sgn <= 0
    i_row, f_row = grow[0:1], _log_sigmoid(grow[1:2])
    i_col, f_col = gcol[:, 0:1], _log_sigmoid(gcol[:, 1:2])
    b_col = jnp.sum(jnp.where(before, f_row, 0.0), axis=1, keepdims=True)
    b_row = jnp.sum(jnp.where(before_t, f_col, 0.0), axis=0, keepdims=True)
    b_last = jnp.sum(f_row, axis=1, keepdims=True)
    r_row = i_row - b_row
    m_prev = state[2] if state is not None else jnp.zeros((1, 1), F32)
    mx_col = jnp.maximum(jnp.max(jnp.where(before, r_row, -jnp.inf), axis=1, keepdims=True), m_prev)
    w = jnp.exp(jnp.where(before, r_row - mx_col, -jnp.inf))
    qs = q * MLSTM_QSCALE
    qb, kb, vb = qs.astype(BF16), k.astype(BF16), v.astype(BF16)
    sqk = _dot_nt(qb, kb) * w
    num = jnp.dot(sqk.astype(BF16), vb, preferred_element_type=F32)
    den = jnp.sum(sqk, axis=1, keepdims=True)
    if state is not None:
        a_col = jnp.exp(m_prev - mx_col)
        num = num + a_col * jnp.dot(qb, state[0].astype(BF16), preferred_element_type=F32)
        den = den + a_col * jnp.sum(qs * state[1], axis=1, keepdims=True)
    h = num * (1.0 / jnp.maximum(jnp.abs(den), jnp.exp(-(b_col + mx_col))))
    g_col = b_last - b_col + i_col
    m_new = jnp.maximum(b_last + m_prev, jnp.max(g_col, axis=0, keepdims=True))
    kw = k * jnp.exp(g_col - m_new)
    c_new = lax.dot_general(kw.astype(BF16), vb, (((0,), (0,)), ((), ())), preferred_element_type=F32)
    n_new = jnp.sum(kw, axis=0, keepdims=True)
    if state is not None:
        decay = jnp.exp(b_last + m_prev - m_new)
        c_new = decay * state[0] + c_new
        n_new = decay * state[1] + n_new
    return h, (c_new, n_new, m_new)


def _mlstm_ctx_body(q_ref, k_ref, v_ref, grow_ref, gcol_ref, h_ref, c_ref, n_ref, m_ref):
    d = pl.program_id(2)
    h, (c, n, m) = _mlstm_chunk(d, q_ref[...], k_ref[...], v_ref[...], grow_ref[...], gcol_ref[...], None)
    h_ref[...] = h
    c_ref[...] = c
    n_ref[...] = n
    m_ref[...] = jnp.broadcast_to(m, m_ref.shape)


def _mlstm_lat_body(m0_ref, q_ref, k_ref, v_ref, grow_ref, gcol_ref, c0_ref, n0_ref, h_ref, c_s, n_s, m_s):
    b, hd, d, c = pl.program_id(0), pl.program_id(1), pl.program_id(2), pl.program_id(3)

    @pl.when(c == 0)
    def _():
        c_s[...] = c0_ref[...]
        n_s[...] = n0_ref[...]
        m_s[...] = jnp.full(m_s.shape, m0_ref[(b * 2 + d) * MLSTM_HEADS + hd], F32)

    state = (c_s[...], n_s[...], m_s[:, 0:1])
    h, (cn, nn, mn) = _mlstm_chunk(d, q_ref[...], k_ref[...], v_ref[...], grow_ref[...], gcol_ref[...], state)
    h_ref[...] = h
    c_s[...] = cn
    n_s[...] = nn
    m_s[...] = jnp.broadcast_to(mn, m_s.shape)


def _mlstm_out_body(hf_ref, hb_ref, o_ref, gn_ref, y_ref):
    for hd in range(MLSTM_HEADS):
        sl = slice(hd * MLSTM_DV, (hd + 1) * MLSTM_DV)
        hs = hf_ref[:, sl] + hb_ref[:, sl]
        hn = hs * lax.rsqrt(jnp.mean(hs * hs, axis=-1, keepdims=True) + LN_EPS) * gn_ref[:, sl]
        y_ref[:, sl] = (jax.nn.sigmoid(o_ref[:, sl]) * hn).astype(BF16)


def _mlstm_mixer(h, win, wgate, bgate, gnorm, state_c, state_n, state_m):
    H, DK, DV = MLSTM_HEADS, MLSTM_DK, MLSTM_DV
    z = _matmul(h, win, 0, name="mlstm_in")
    g = _matmul(h, wgate, 0, bgate, tm=ROW_TILE, name="mlstm_gate")
    g4 = g.reshape(T, 2, 2, H)
    grow = jnp.transpose(g4, (1, 3, 2, 0))
    gcol = jnp.transpose(g4, (1, 3, 0, 2))

    def specs(row_blk):
        return [pl.BlockSpec((ML, DK), lambda *a: (row_blk(*a), a[1])),
                pl.BlockSpec((ML, DK), lambda *a: (row_blk(*a), H + a[1])),
                pl.BlockSpec((ML, DV), lambda *a: (row_blk(*a), H + a[1])),
                pl.BlockSpec((None, None, 2, ML), lambda *a: (a[2], a[1], 0, row_blk(*a))),
                pl.BlockSpec((None, None, ML, 2), lambda *a: (a[2], a[1], row_blk(*a), 0))]

    def hspec(row_blk):
        return pl.BlockSpec((None, ML, DV), lambda *a: (a[2], row_blk(*a), a[1]))

    assert SEQ == ML
    st = lambda shape: pl.BlockSpec((None, None, None) + shape, lambda s, hd, d: (s, d, hd, 0, 0))
    h_ctx, c_new, n_new, m_new = pl.pallas_call(
        _mlstm_ctx_body,
        out_shape=(jax.ShapeDtypeStruct((2, TP, NV), F32),
                   jax.ShapeDtypeStruct((BATCH, 2, H, DK, DV), F32),
                   jax.ShapeDtypeStruct((BATCH, 2, H, 1, DK), F32),
                   jax.ShapeDtypeStruct((BATCH, 2, H, 1, 128), F32)),
        grid=(BATCH, H, 2),
        in_specs=specs(lambda s, hd, d: s),
        out_specs=(hspec(lambda s, hd, d: s), st((DK, DV)), st((1, DK)), st((1, 128))),
        compiler_params=_cp("arbitrary", "arbitrary", "arbitrary"),
        name="mlstm_ctx")(z, z, z, grow, gcol)

    nc = DEC_SEQ // ML
    base = TP // ML

    def lat_blk(b, hd, d, c):
        return base + b * nc + jnp.where(d == 0, c, nc - 1 - c)

    lst = lambda shape: pl.BlockSpec((None, None, None) + shape, lambda b, hd, d, c: (b, d, hd, 0, 0))
    h_lat = pl.pallas_call(
        _mlstm_lat_body,
        out_shape=jax.ShapeDtypeStruct((2, TS, NV), F32),
        grid=(DEC_BATCH, H, 2, nc),
        in_specs=[pl.BlockSpec(memory_space=pltpu.SMEM)] + specs(lat_blk) + [lst((DK, DV)), lst((1, DK))],
        out_specs=pl.BlockSpec((None, ML, DV), lambda b, hd, d, c: (d, lat_blk(b, hd, d, c) - base, hd)),
        scratch_shapes=[pltpu.VMEM((DK, DV), F32), pltpu.VMEM((1, DK), F32), pltpu.VMEM((1, 128), F32)],
        compiler_params=_cp("arbitrary", "arbitrary", "arbitrary", "arbitrary"),
        name="mlstm_lat")(state_m.reshape(-1), z, z, z, grow, gcol, state_c,
                          state_n.reshape(DEC_BATCH, 2, H, 1, DK))

    hh = jnp.concatenate([h_ctx, h_lat], axis=1)
    row = lambda j: pl.BlockSpec((ROW_TILE, NV), lambda i: (i, j))
    y = pl.pallas_call(
        _mlstm_out_body,
        out_shape=jax.ShapeDtypeStruct((T, NV), BF16),
        grid=(T // ROW_TILE,),
        in_specs=[pl.BlockSpec((None, ROW_TILE, NV), lambda i: (0, i, 0)),
                  pl.BlockSpec((None, ROW_TILE, NV), lambda i: (1, i, 0)),
                  row((2 * NQK + NV) // NV), pl.BlockSpec((1, NV), lambda i: (0, 0))],
        out_specs=row(0),
        compiler_params=_cp("arbitrary"),
        name="mlstm_out")(hh, hh, z, gnorm.reshape(1, NV))
    new_state = (c_new, n_new.reshape(BATCH, 2, H, DK), m_new[:, :, :, 0, 0])
    return y, new_state


CONV_HALO = 16
CONV_RT = 64
assert SEQ == ROW_TILE and DEC_SEQ % ROW_TILE == 0


def _conv_body(a_ref, g_ref, ap_ref, gp_ref, an_ref, gn_ref, dw_ref, dwb_ref, lng_ref, lnb_ref, y_ref, u_ref, c_ref):
    i = pl.program_id(0)
    tiles_per_seq = DEC_SEQ // ROW_TILE
    pos = (i - TP // ROW_TILE) % tiles_per_seq
    is_lat = i >= TP // ROW_TILE
    has_prev = is_lat & (pos > 0)
    has_next = is_lat & (pos < tiles_per_seq - 1)

    def glu(a, g):
        return a * jax.nn.sigmoid(g)

    H = CONV_HALO
    u_ref[0:H, :] = jnp.where(has_prev, glu(ap_ref[...], gp_ref[...]), 0.0)
    u_ref[H:H + ROW_TILE, :] = glu(a_ref[...], g_ref[...])
    u_ref[H + ROW_TILE:, :] = jnp.where(has_next, glu(an_ref[...], gn_ref[...]), 0.0)

    off = H - CONV_WIDTH // 2

    def lane_chunk(c, carry):
        c0 = pl.multiple_of(c * 128, 128)
        for r in range(ROW_TILE // CONV_RT):
            acc = jnp.zeros((CONV_RT, 128), F32)
            for j in range(CONV_WIDTH):
                acc = acc + u_ref[pl.ds(r * CONV_RT + off + j, CONV_RT), pl.ds(c0, 128)] * dw_ref[j:j + 1, pl.ds(c0, 128)]
            c_ref[pl.ds(r * CONV_RT, CONV_RT), pl.ds(c0, 128)] = acc
        return carry

    lax.fori_loop(0, D_MODEL // 128, lane_chunk, 0)
    v = c_ref[...] + dwb_ref[...]
    mu = jnp.mean(v, axis=-1, keepdims=True)
    d = v - mu
    var = jnp.mean(d * d, axis=-1, keepdims=True)
    xn = d * lax.rsqrt(var + LN_EPS) * lng_ref[...] + lnb_ref[...]
    y_ref[...] = (xn * jax.nn.sigmoid(xn)).astype(BF16)


def _conv_mixer(h, w1, b1, dw, dwb, lng, lnb):
    z = _matmul(h, w1, 0, b1, name="conv_in")
    D = D_MODEL
    hb = ROW_TILE // CONV_HALO
    nh = T // CONV_HALO
    cur = lambda j: pl.BlockSpec((ROW_TILE, D), lambda i: (i, j))
    prev = lambda j: pl.BlockSpec((CONV_HALO, D), lambda i: (jnp.maximum(i * hb - 1, 0), j))
    nxt = lambda j: pl.BlockSpec((CONV_HALO, D), lambda i: (jnp.minimum((i + 1) * hb, nh - 1), j))
    vec = pl.BlockSpec((None, 1, D), lambda i: (0, 0, 0))
    return pl.pallas_call(
        _conv_body,
        out_shape=jax.ShapeDtypeStruct((T, D), BF16),
        grid=(T // ROW_TILE,),
        in_specs=[cur(0), cur(1), prev(0), prev(1), nxt(0), nxt(1),
                  pl.BlockSpec((None, CONV_WIDTH, D), lambda i: (0, 0, 0)), vec, vec, vec],
        out_specs=cur(0),
        scratch_shapes=[pltpu.VMEM((ROW_TILE + 2 * CONV_HALO, D), F32), pltpu.VMEM((ROW_TILE, D), F32)],
        compiler_params=_cp("arbitrary"),
        name="conv_mid")(z, z, z, z, z, z, dw, dwb.reshape(-1, 1, D), lng.reshape(-1, 1, D), lnb.reshape(-1, 1, D))


MOE_TM = 512
MOE_NT = T * TOP_K // MOE_TM + N_EXPERTS
MOE_UC = 512
MOE_TC = MOE_UC // 2
MOE_N1 = D_EXPERT // MOE_TC
PERM_W = 256
MOE_TN = 1024
MOE_N2 = D_MODEL // MOE_TN
MOE_CT = 128
TOPK_TILE = 1024


def _topk_body(l_ref, idx_ref, gate_ref):
    l = l_ref[...]
    lane = lax.broadcasted_iota(I32, l.shape, 1)
    vals, idxs = [], []
    for _ in range(TOP_K):
        m = jnp.max(l, axis=-1, keepdims=True)
        ix = jnp.min(jnp.where(l == m, lane, N_EXPERTS), axis=-1, keepdims=True)
        vals.append(m)
        idxs.append(ix)
        l = jnp.where(lane == ix, -jnp.inf, l)
    e = [jnp.exp(v - vals[0]) for v in vals]
    inv = 1.0 / functools.reduce(jnp.add, e)
    kcol = lax.broadcasted_iota(I32, idx_ref.shape, 1)
    idx = jnp.zeros(idx_ref.shape, I32)
    gate = jnp.zeros(gate_ref.shape, F32)
    for k in range(TOP_K):
        idx = jnp.where(kcol == k, idxs[k], idx)
        gate = jnp.where(kcol == k, e[k] * inv, gate)
    idx_ref[...] = idx
    gate_ref[...] = gate


def _topk(logits):
    spec = lambda w: pl.BlockSpec((TOPK_TILE, w), lambda i: (i, 0))
    return pl.pallas_call(
        _topk_body,
        out_shape=(jax.ShapeDtypeStruct((T, TOP_K), I32), jax.ShapeDtypeStruct((T, TOP_K), F32)),
        grid=(T // TOPK_TILE,),
        in_specs=[spec(N_EXPERTS)],
        out_specs=(spec(TOP_K), spec(TOP_K)),
        compiler_params=_cp("arbitrary"),
        name="moe_topk")(logits)


def _moe_plan(top_i):
    A = T * TOP_K
    flat_e = top_i.reshape(A)
    onehot = (flat_e[:, None] == jnp.arange(N_EXPERTS, dtype=I32)[None, :]).astype(I32)
    rank = jnp.take_along_axis(jnp.cumsum(onehot, axis=0), flat_e[:, None], axis=1)[:, 0] - 1
    counts = jnp.sum(onehot, axis=0)
    padded = (counts + MOE_TM - 1) // MOE_TM * MOE_TM
    pad_end = jnp.cumsum(padded)
    dest = (pad_end - padded)[flat_e] + rank
    slot_tok = jnp.zeros((MOE_NT * MOE_TM,), I32).at[dest].set(jnp.arange(A, dtype=I32) // TOP_K)
    tile_start = jnp.arange(MOE_NT, dtype=I32) * MOE_TM
    tile_e = jnp.minimum(jnp.searchsorted(pad_end, tile_start, side='right'), N_EXPERTS - 1).astype(I32)
    tile_on = (tile_start < pad_end[-1]).astype(I32)
    return slot_tok, tile_e, tile_on, dest.astype(I32)


def _deinterleave(u):
    r = lax.broadcasted_iota(I32, (PERM_W, PERM_W), 0)
    c = lax.broadcasted_iota(I32, (PERM_W, PERM_W), 1)
    half = PERM_W // 2
    src = jnp.where(c < half, 2 * c, 2 * (c - half) + 1)
    sel = jnp.where(r == src, 1.0, 0.0).astype(BF16)
    hi = u.astype(BF16)
    r1 = u - hi.astype(F32)
    mid = r1.astype(BF16)
    lo = (r1 - mid.astype(F32)).astype(BF16)
    out = (jnp.dot(hi, sel, preferred_element_type=F32) + jnp.dot(mid, sel, preferred_element_type=F32)
           + jnp.dot(lo, sel, preferred_element_type=F32))
    return out[:, :half], out[:, half:]


def _gmm_body(te_ref, on_ref, tok_ref, tokn_ref, x_hbm, w1_ref, b1_ref, w2_ref, b2_ref,
              y_ref, xg_ref, xbf_ref, act_ref, sem):
    t, j = pl.program_id(0), pl.program_id(1)
    on = on_ref[t] != 0

    def row_copy(src_row, r):
        return pltpu.make_async_copy(x_hbm.at[pl.ds(src_row, 1), :], xg_ref.at[pl.ds(r, 1), :], sem)

    def issue_rows(tk_ref):
        def issue(r, c):
            row_copy(tk_ref[0, r], r).start()
            return c

        lax.fori_loop(0, MOE_TM, issue, 0, unroll=8)

    @pl.when(on & (j == 0))
    def _():
        @pl.when(t == 0)
        def _():
            issue_rows(tok_ref)

        def wait(r, c):
            row_copy(0, r).wait()
            return c

        lax.fori_loop(0, MOE_TM, wait, 0, unroll=8)
        xbf_ref[...] = xg_ref[...].astype(BF16)

        nxt = jnp.minimum(t + 1, MOE_NT - 1)

        @pl.when((t + 1 < MOE_NT) & (on_ref[nxt] != 0))
        def _():
            issue_rows(tokn_ref)

    for jj in range(MOE_N1):
        @pl.when(on & (j == jj))
        def _():
            u = jnp.dot(xbf_ref[...], w1_ref[...].astype(BF16), preferred_element_type=F32) + b1_ref[...]
            acts = []
            for blk in range(MOE_UC // PERM_W):
                glu, lin = _deinterleave(u[:, blk * PERM_W:(blk + 1) * PERM_W])
                glu = jnp.minimum(glu, SWIGLU_LIMIT)
                lin = jnp.clip(lin, -SWIGLU_LIMIT, SWIGLU_LIMIT)
                acts.append((glu * jax.nn.sigmoid(SWIGLU_ALPHA * glu) * (lin + 1.0)).astype(BF16))
            act_ref[jj] = jnp.concatenate(acts, axis=1)

    @pl.when(on & (j >= MOE_N1))
    def _():
        acc = jnp.broadcast_to(b2_ref[...], y_ref.shape)
        for jj in range(MOE_N1):
            w2 = w2_ref[jj * MOE_TC:(jj + 1) * MOE_TC, :].astype(BF16)
            acc = acc + jnp.dot(act_ref[jj], w2, preferred_element_type=F32)
        y_ref[...] = acc

    @pl.when(jnp.logical_not(on) & (j >= MOE_N1))
    def _():
        y_ref[...] = jnp.zeros(y_ref.shape, F32)


def _gmm(x, slot_tok, tile_e, tile_on, w1, b1, w2, b2, lidx):
    D, DE = D_MODEL, D_EXPERT
    c1 = lambda j: jnp.minimum(j, MOE_N1 - 1)
    c2 = lambda j: jnp.maximum(j - MOE_N1, 0)
    tok3 = slot_tok.reshape(MOE_NT, 1, MOE_TM)
    tok_spec = lambda nxt: pl.BlockSpec((None, 1, MOE_TM),
                                        lambda t, j, te, on: (jnp.minimum(t + nxt, MOE_NT - 1), 0, 0),
                                        memory_space=pltpu.SMEM)
    grid_spec = pltpu.PrefetchScalarGridSpec(
        num_scalar_prefetch=2,
        grid=(MOE_NT, MOE_N1 + MOE_N2),
        in_specs=[tok_spec(0), tok_spec(1),
                  pl.BlockSpec(memory_space=pl.ANY),
                  pl.BlockSpec((None, None, D, MOE_UC), lambda t, j, te, on: (lidx, te[t], 0, c1(j))),
                  pl.BlockSpec((None, None, 1, MOE_UC), lambda t, j, te, on: (lidx, te[t], 0, c1(j))),
                  pl.BlockSpec((None, None, DE, MOE_TN), lambda t, j, te, on: (lidx, te[t], 0, c2(j))),
                  pl.BlockSpec((None, None, 1, MOE_TN), lambda t, j, te, on: (lidx, te[t], 0, c2(j)))],
        out_specs=pl.BlockSpec((MOE_TM, MOE_TN), lambda t, j, te, on: (t, c2(j))),
        scratch_shapes=[pltpu.VMEM((MOE_TM, D), F32), pltpu.VMEM((MOE_TM, D), BF16),
                        pltpu.VMEM((MOE_N1, MOE_TM, MOE_TC), BF16), pltpu.SemaphoreType.DMA(())])
    return pl.pallas_call(
        _gmm_body,
        out_shape=jax.ShapeDtypeStruct((MOE_NT * MOE_TM, D), F32),
        grid_spec=grid_spec,
        compiler_params=_cp("arbitrary", "arbitrary"),
        name="moe_gmm")(tile_e, tile_on, tok3, tok3, x, w1, b1.reshape(b1.shape[0], N_EXPERTS, 1, 2 * DE),
                        w2, b2.reshape(b2.shape[0], N_EXPERTS, 1, D))


def _combine_body(slot_ref, gate_ref, y_hbm, o_ref, buf_ref, sem):
    def row_copy(r, k):
        return pltpu.make_async_copy(y_hbm.at[pl.ds(slot_ref[0, r * TOP_K + k], 1), :],
                                     buf_ref.at[k, pl.ds(r, 1), :], sem)

    def issue(r, c):
        for k in range(TOP_K):
            row_copy(r, k).start()
        return c

    lax.fori_loop(0, MOE_CT, issue, 0)

    def wait(r, c):
        for k in range(TOP_K):
            row_copy(r, k).wait()
        return c

    lax.fori_loop(0, MOE_CT, wait, 0)
    g = gate_ref[...]
    acc = buf_ref[0] * g[:, 0:1]
    for k in range(1, TOP_K):
        acc = acc + buf_ref[k] * g[:, k:k + 1]
    o_ref[...] = acc


def _combine(yb, slot_of, gates):
    n = T // MOE_CT
    return pl.pallas_call(
        _combine_body,
        out_shape=jax.ShapeDtypeStruct((T, D_MODEL), F32),
        grid=(n,),
        in_specs=[pl.BlockSpec((None, 1, MOE_CT * TOP_K), lambda i: (i, 0, 0), memory_space=pltpu.SMEM),
                  pl.BlockSpec((MOE_CT, TOP_K), lambda i: (i, 0)),
                  pl.BlockSpec(memory_space=pl.ANY)],
        out_specs=pl.BlockSpec((MOE_CT, D_MODEL), lambda i: (i, 0)),
        scratch_shapes=[pltpu.VMEM((TOP_K, MOE_CT, D_MODEL), F32), pltpu.SemaphoreType.DMA(())],
        compiler_params=_cp("arbitrary"),
        name="moe_combine")(slot_of.reshape(n, 1, MOE_CT * TOP_K), gates, yb)


def _moe(h, wr, br, w1, b1, w2, b2, lidx):
    logits = _matmul(h, wr, lidx, br, tm=ROW_TILE, name="moe_router")
    top_i, gates = _topk(logits)
    slot_tok, tile_e, tile_on, slot_of = _moe_plan(top_i)
    yb = _gmm(h, slot_tok, tile_e, tile_on, w1, b1, w2, b2, lidx)
    return _combine(yb, slot_of, gates)


def kernel(x_prompt, x_sample, cache_attn_k, cache_attn_v, cache_na_k, cache_na_v, state_mlstm_C, state_mlstm_n, state_mlstm_m, c, c_ctx, ada_w, ada_b, ln1_g, ln1_b, ln2_g, ln2_b, attn_wqkv, attn_wo, attn_sink, na_wqkv, na_wo, na_rpb, mlstm_win, mlstm_wgate, mlstm_bgate, mlstm_gnorm, mlstm_wo, conv_w1, conv_b1, conv_dw, conv_dwb, conv_ln_g, conv_ln_b, conv_w2, conv_b2, moe_wr, moe_br, moe_w1, moe_b1, moe_w2, moe_b2):
    D = D_MODEL
    x = jnp.concatenate([x_prompt.reshape(TP, D), x_sample.reshape(TS, D)], axis=0)
    cond = jnp.concatenate([c_ctx[None, :], c, jnp.zeros((ADA_ROWS - 1 - DEC_BATCH, D), F32)], axis=0)
    mods = _ada(cond, ada_w, ada_b)[:, :N_SEG].reshape(DEPTH, N_SEG, 6, D)

    h = _modulate(x, mods[0])
    outs = {}
    for i in range(DEPTH):
        mixer = i % 4
        if mixer == 0:
            o, z = _attn_mixer(h, attn_wqkv, attn_sink[0], cache_attn_k[:, 0], cache_attn_v[:, 0])
            kv = z[:TP, NQ_COLS:].reshape(BATCH, 1, SEQ, 2, ATTN_KV_HEADS, ATTN_HEAD_DIM)
            outs['ak'], outs['av'] = kv[:, :, :, 0], kv[:, :, :, 1]
            y = _matmul(o, attn_wo, 0, name="attn_out")
        elif mixer == 1:
            o, z = _na_mixer(h, na_wqkv, na_rpb[0], cache_na_k[:, 0], cache_na_v[:, 0])
            kv = z[:TP, NA_HEADS * NA_HEAD_DIM:].reshape(BATCH, 1, SEQ, 2, NA_HEADS, NA_HEAD_DIM)
            outs['nk'], outs['nv'] = kv[:, :, :, 0], kv[:, :, :, 1]
            y = _matmul(o, na_wo, 0, name="na_out")
        elif mixer == 2:
            o, st = _mlstm_mixer(h, mlstm_win, mlstm_wgate, mlstm_bgate, mlstm_gnorm[0],
                                 state_mlstm_C[:, 0], state_mlstm_n[:, 0], state_mlstm_m[:, 0])
            outs['C'], outs['n'], outs['m'] = (s[:, None] for s in st)
            y = _matmul(o, mlstm_wo, 0, name="mlstm_out_proj")
        else:
            o = _conv_mixer(h, conv_w1, conv_b1, conv_dw[0:1], conv_dwb[0:1], conv_ln_g[0:1], conv_ln_b[0:1])
            y = _matmul(o, conv_w2, 0, conv_b2, name="conv_out")
        x, h2 = _ln_mod(x, y, mods[i], ln1_g, ln1_b, i, 2, mods[i], 3, h_dtype=F32)
        f = _moe(h2, moe_wr, moe_br, moe_w1, moe_b1, moe_w2, moe_b2, i)
        if i + 1 < DEPTH:
            x, h = _ln_mod(x, f, mods[i], ln2_g, ln2_b, i, 5, mods[i + 1], 0)
        else:
            x = _ln_mod(x, f, mods[i], ln2_g, ln2_b, i, 5)
    return (x[:TP].reshape(BATCH, SEQ, D), x[TP:].reshape(DEC_BATCH, DEC_SEQ, D),
            outs['ak'], outs['av'], outs['nk'], outs['nv'], outs['C'], outs['n'], outs['m'])
```

```python
import functools

import jax
import jax.numpy as jnp
from jax import lax
from jax.experimental import pallas as pl
from jax.experimental.pallas import tpu as pltpu

F32 = jnp.float32
BF16 = jnp.bfloat16
I32 = jnp.int32

D_MODEL = 4096
BATCH = 16
SEQ = 256
DEPTH = 4
DEC_BATCH = 2
DEC_SEQ = 4096
PAST_LEN = 256
GRID_W = 64
DN_ALPHA = (2 * DEPTH) ** 0.25
LN_EPS = 1e-5
NEG_INF = -1e30

ATTN_HEADS = 64
ATTN_KV_HEADS = 8
ATTN_HEAD_DIM = 64
ATTN_GROUP = ATTN_HEADS // ATTN_KV_HEADS
ATTN_WINDOW = 128
ROPE_BASE = 10000.0

NA_HEADS = 32
NA_HEAD_DIM = 128
NA_ROWS = 8
NA_COLS = 16

MLSTM_HEADS = 8
MLSTM_DK = 256
MLSTM_DV = 512

CONV_WIDTH = 31

N_EXPERTS = 32
TOP_K = 4
D_EXPERT = 1024
SWIGLU_LIMIT = 7.0
SWIGLU_ALPHA = 1.702

TP = BATCH * SEQ
TS = DEC_BATCH * DEC_SEQ
T = TP + TS
SEG = DEC_SEQ
N_SEG = T // SEG
assert TP == SEG and T % SEG == 0

V7X_VMEM_LIMIT_BYTES = 56 * 2**20


def _cp(*sem):
    return pltpu.CompilerParams(dimension_semantics=sem, vmem_limit_bytes=V7X_VMEM_LIMIT_BYTES)


def _mm_body(x_ref, w_ref, b_ref, o_ref, wbf_ref):
    @pl.when(pl.program_id(1) == 0)
    def _():
        wbf_ref[...] = w_ref[...].astype(BF16)

    acc = jnp.dot(x_ref[...].astype(BF16), wbf_ref[...], preferred_element_type=F32)
    if b_ref is not None:
        acc = acc + b_ref[...]
    o_ref[...] = acc.astype(o_ref.dtype)


def _mm_body_nobias(x_ref, w_ref, o_ref, wbf_ref):
    _mm_body(x_ref, w_ref, None, o_ref, wbf_ref)


def _matmul(x, w, widx, b=None, *, out_dtype=F32, tm=1024, tn=512, name):
    M, K = x.shape
    N = w.shape[2]
    tm, tn = min(tm, M), min(tn, N)
    assert M % tm == 0 and N % tn == 0
    in_specs = [pl.BlockSpec((tm, K), lambda j, i: (i, 0)),
                pl.BlockSpec((None, K, tn), lambda j, i: (widx, 0, j))]
    args = [x, w]
    if b is not None:
        in_specs.append(pl.BlockSpec((None, 1, tn), lambda j, i: (widx, 0, j)))
        args.append(b.reshape(b.shape[0], 1, N))
    return pl.pallas_call(
        _mm_body if b is not None else _mm_body_nobias,
        out_shape=jax.ShapeDtypeStruct((M, N), out_dtype),
        grid=(N // tn, M // tm),
        in_specs=in_specs,
        out_specs=pl.BlockSpec((tm, tn), lambda j, i: (i, j)),
        scratch_shapes=[pltpu.VMEM((K, tn), BF16)],
        compiler_params=_cp("arbitrary", "arbitrary"),
        name=name)(*args)


ADA_ROWS = 16
ADA_TN = 512


def _ada_body(c_ref, w_ref, b_ref, o_ref):
    c = c_ref[...]
    s = (c * jax.nn.sigmoid(c)).astype(BF16)
    o_ref[...] = jnp.dot(s, w_ref[...].astype(BF16), preferred_element_type=F32) + b_ref[...]


def _ada(cond, ada_w, ada_b):
    L, D, N = ada_w.shape
    return pl.pallas_call(
        _ada_body,
        out_shape=jax.ShapeDtypeStruct((L, ADA_ROWS, N), F32),
        grid=(L, N // ADA_TN),
        in_specs=[pl.BlockSpec((ADA_ROWS, D), lambda l, j: (0, 0)),
                  pl.BlockSpec((None, D, ADA_TN), lambda l, j: (l, 0, j)),
                  pl.BlockSpec((None, 1, ADA_TN), lambda l, j: (l, 0, j))],
        out_specs=pl.BlockSpec((None, ADA_ROWS, ADA_TN), lambda l, j: (l, 0, j)),
        compiler_params=_cp("arbitrary", "arbitrary"),
        name="ada")(cond, ada_w, ada_b.reshape(L, 1, N))


ROW_TILE = 256


def _seg_spec():
    return pl.BlockSpec((None, 6, D_MODEL), lambda i: (i // (SEG // ROW_TILE), 0, 0))


def _modulate_body(x_ref, m_ref, h_ref):
    m = m_ref[...]
    h_ref[...] = (x_ref[...] * (1.0 + m[1:2]) + m[0:1]).astype(BF16)


def _modulate(x, mods):
    return pl.pallas_call(
        _modulate_body,
        out_shape=jax.ShapeDtypeStruct((T, D_MODEL), BF16),
        grid=(T // ROW_TILE,),
        in_specs=[pl.BlockSpec((ROW_TILE, D_MODEL), lambda i: (i, 0)), _seg_spec()],
        out_specs=pl.BlockSpec((ROW_TILE, D_MODEL), lambda i: (i, 0)),
        compiler_params=_cp("arbitrary"),
        name="modulate")(x, mods)


def _ln_mod_body(x_ref, y_ref, m_ref, g_ref, b_ref, mn_ref, xo_ref, ho_ref, *, gate_row, shift_row, h_dtype):
    m = m_ref[...]
    v = DN_ALPHA * x_ref[...] + (1.0 + m[gate_row:gate_row + 1]) * y_ref[...]
    mu = jnp.mean(v, axis=-1, keepdims=True)
    d = v - mu
    var = jnp.mean(d * d, axis=-1, keepdims=True)
    xn = d * lax.rsqrt(var + LN_EPS) * g_ref[...] + b_ref[...]
    xo_ref[...] = xn
    if ho_ref is not None:
        mn = mn_ref[...]
        ho_ref[...] = (xn * (1.0 + mn[shift_row + 1:shift_row + 2]) + mn[shift_row:shift_row + 1]).astype(h_dtype)


def _ln_body_last(x_ref, y_ref, m_ref, g_ref, b_ref, xo_ref, *, gate_row):
    _ln_mod_body(x_ref, y_ref, m_ref, g_ref, b_ref, None, xo_ref, None, gate_row=gate_row, shift_row=0, h_dtype=None)


def _ln_mod(x, y, mods, g, b, lidx, gate_row, mods_next=None, shift_row=0, h_dtype=BF16):
    row = pl.BlockSpec((ROW_TILE, D_MODEL), lambda i: (i, 0))
    vec = pl.BlockSpec((None, 1, D_MODEL), lambda i: (lidx, 0, 0))
    in_specs = [row, row, _seg_spec(), vec, vec]
    args = [x, y, mods, g.reshape(-1, 1, D_MODEL), b.reshape(-1, 1, D_MODEL)]
    if mods_next is None:
        body = functools.partial(_ln_body_last, gate_row=gate_row)
        out_shape = jax.ShapeDtypeStruct((T, D_MODEL), F32)
        out_specs = row
    else:
        body = functools.partial(_ln_mod_body, gate_row=gate_row, shift_row=shift_row, h_dtype=h_dtype)
        in_specs.append(_seg_spec())
        args.append(mods_next)
        out_shape = (jax.ShapeDtypeStruct((T, D_MODEL), F32), jax.ShapeDtypeStruct((T, D_MODEL), h_dtype))
        out_specs = (row, row)
    return pl.pallas_call(body, out_shape=out_shape, grid=(T // ROW_TILE,), in_specs=in_specs,
                          out_specs=out_specs, compiler_params=_cp("arbitrary"), name="ln_mod")(*args)


def _dot_nt(a, b):
    return lax.dot_general(a, b, (((1,), (1,)), ((), ())), preferred_element_type=F32)


def _softmax_pv(scores, values, sink_col=None):
    m = functools.reduce(jnp.maximum, [jnp.max(s, axis=-1, keepdims=True) for s in scores])
    if sink_col is not None:
        m = jnp.maximum(m, sink_col)
    den = jnp.exp(sink_col - m) if sink_col is not None else 0.0
    out = None
    for s, v in zip(scores, values):
        e = jnp.exp(s - m)
        den = den + jnp.sum(e, axis=-1, keepdims=True)
        pv = jnp.dot(e.astype(BF16), v, preferred_element_type=F32)
        out = pv if out is None else out + pv
    return out * (1.0 / den)


QB = 128
NQB = T // QB
ATTN_SCALE = ATTN_HEAD_DIM ** -0.5
NQ_COLS = ATTN_HEADS * ATTN_HEAD_DIM
NKV_COLS = ATTN_KV_HEADS * ATTN_HEAD_DIM


def _rope_tables():
    lane = jnp.arange(128)
    hd = lane % ATTN_HEAD_DIM
    n = ATTN_HEAD_DIM // 4
    inv = ROPE_BASE ** (-(hd % n).astype(F32) / n)
    t = jnp.arange(DEC_SEQ)
    row = (t // GRID_W).astype(F32)
    col = (t % GRID_W).astype(F32)
    pos = jnp.where((hd < ATTN_HEAD_DIM // 2)[None, :], row[:, None], col[:, None])
    ang = pos * inv[None, :]
    sign = jnp.where((hd % (2 * n)) < n, -1.0, 1.0).astype(F32)
    cos = jnp.concatenate([jnp.ones((TP, 128), F32)] + [jnp.cos(ang)] * DEC_BATCH)
    sin = jnp.concatenate([jnp.zeros((TP, 128), F32)] + [jnp.sin(ang) * sign[None, :]] * DEC_BATCH)
    return cos, sin


def _qkv_rope_body(z_ref, cos_ref, sin_ref, q_ref, k_ref, v_ref):
    cos = cos_ref[...]
    sin = sin_ref[...]
    lane = lax.broadcasted_iota(I32, (QB, 128), 1)
    first = (lane % 32) < 16

    def rope(x):
        xr = jnp.where(first, pltpu.roll(x, 112, 1), pltpu.roll(x, 16, 1))
        return x * cos + xr * sin

    for c in range(NQ_COLS // 128):
        y = (rope(z_ref[:, c * 128:(c + 1) * 128]) * ATTN_SCALE).astype(BF16)
        kvh, g = (2 * c) // ATTN_GROUP, (2 * c) % ATTN_GROUP
        q_ref[kvh, g] = y[:, :64]
        q_ref[kvh, g + 1] = y[:, 64:]
    for c in range(NKV_COLS // 128):
        c0 = NQ_COLS + c * 128
        y = rope(z_ref[:, c0:c0 + 128]).astype(BF16)
        k_ref[2 * c] = y[:, :64]
        k_ref[2 * c + 1] = y[:, 64:]
        c0 = NQ_COLS + NKV_COLS + c * 128
        y = z_ref[:, c0:c0 + 128].astype(BF16)
        v_ref[2 * c] = y[:, :64]
        v_ref[2 * c + 1] = y[:, 64:]


def _qkv_rope(z, cos, sin):
    ncol = z.shape[1]
    kv_shape = jax.ShapeDtypeStruct((ATTN_KV_HEADS, T, ATTN_HEAD_DIM), BF16)
    kv_spec = pl.BlockSpec((ATTN_KV_HEADS, QB, ATTN_HEAD_DIM), lambda i: (0, i, 0))
    tab = pl.BlockSpec((QB, 128), lambda i: (i, 0))
    return pl.pallas_call(
        _qkv_rope_body,
        out_shape=(jax.ShapeDtypeStruct((NQB, ATTN_KV_HEADS, ATTN_GROUP, QB, ATTN_HEAD_DIM), BF16), kv_shape, kv_shape),
        grid=(NQB,),
        in_specs=[pl.BlockSpec((QB, ncol), lambda i: (i, 0)), tab, tab],
        out_specs=(pl.BlockSpec((None, ATTN_KV_HEADS, ATTN_GROUP, QB, ATTN_HEAD_DIM), lambda i: (i, 0, 0, 0, 0)),
                   kv_spec, kv_spec),
        compiler_params=_cp("arbitrary"),
        name="qkv_rope")(z, cos, sin)


def _sink_col(sink_ref, kvh, reps):
    cols = [jnp.full((QB, 1), sink_ref[kvh * ATTN_GROUP + g], F32) for g in range(ATTN_GROUP)]
    return jnp.concatenate(cols * reps, axis=0)


def _store_heads(o_ref, o, blk):
    for gp in range(ATTN_GROUP // 2):
        a = o[(blk * ATTN_GROUP + 2 * gp) * QB:(blk * ATTN_GROUP + 2 * gp + 1) * QB]
        b = o[(blk * ATTN_GROUP + 2 * gp + 1) * QB:(blk * ATTN_GROUP + 2 * gp + 2) * QB]
        o_ref[blk * QB:(blk + 1) * QB, gp * 128:(gp + 1) * 128] = jnp.concatenate([a, b], axis=1)


def _attn_ctx_body(sink_ref, q_ref, k_ref, v_ref, o_ref):
    kvh = pl.program_id(1)
    nblk = SEQ // QB
    q = q_ref[...].reshape(nblk * ATTN_GROUP * QB, ATTN_HEAD_DIM)
    s = _dot_nt(q, k_ref[...])
    o = _softmax_pv([s], [v_ref[...]], _sink_col(sink_ref, kvh, nblk)).astype(BF16)
    for blk in range(nblk):
        _store_heads(o_ref, o, blk)


def _attn_lat_body(sink_ref, q_ref, kp_ref, kc_ref, kn_ref, vp_ref, vc_ref, vn_ref, kx_ref, vx_ref, o_ref):
    kvh = pl.program_id(1)
    i = pl.program_id(2)
    q = q_ref[...].reshape(ATTN_GROUP * QB, ATTN_HEAD_DIM)
    k_loc = jnp.concatenate([kp_ref[...], kc_ref[...], kn_ref[...]], axis=0)
    v_loc = jnp.concatenate([vp_ref[...], vc_ref[...], vn_ref[...]], axis=0)
    s_loc = _dot_nt(q, k_loc).reshape(ATTN_GROUP, QB, 3 * QB)
    qpos = i * QB + lax.broadcasted_iota(I32, (QB, 3 * QB), 0)
    kpos = (i - 1) * QB + lax.broadcasted_iota(I32, (QB, 3 * QB), 1)
    ok = (jnp.abs(qpos - kpos) <= ATTN_WINDOW) & (kpos >= 0) & (kpos < DEC_SEQ)
    s_loc = jnp.where(ok[None], s_loc, NEG_INF).reshape(ATTN_GROUP * QB, 3 * QB)
    s_ctx = _dot_nt(q, kx_ref[...].astype(BF16))
    o = _softmax_pv([s_loc, s_ctx], [v_loc, vx_ref[...].astype(BF16)], _sink_col(sink_ref, kvh, 1))
    _store_heads(o_ref, o.astype(BF16), 0)


def _attn_mixer(h, wqkv, sink, cache_k, cache_v):
    z = _matmul(h, wqkv, 0, name="attn_qkv")
    cos, sin = _rope_tables()
    q5, k3, v3 = _qkv_rope(z, cos, sin)
    smem = pl.BlockSpec(memory_space=pltpu.SMEM)
    G, dh, KVH = ATTN_GROUP, ATTN_HEAD_DIM, ATTN_KV_HEADS
    nblk = SEQ // QB
    o_ctx = pl.pallas_call(
        _attn_ctx_body,
        out_shape=jax.ShapeDtypeStruct((TP, NQ_COLS), BF16),
        grid=(BATCH, KVH),
        in_specs=[smem,
                  pl.BlockSpec((nblk, None, G, QB, dh), lambda b, h_: (b, h_, 0, 0, 0)),
                  pl.BlockSpec((None, SEQ, dh), lambda b, h_: (h_, b, 0)),
                  pl.BlockSpec((None, SEQ, dh), lambda b, h_: (h_, b, 0))],
        out_specs=pl.BlockSpec((SEQ, G * dh), lambda b, h_: (b, h_)),
        compiler_params=_cp("arbitrary", "arbitrary"),
        name="attn_ctx")(sink, q5, k3, v3)

    nq = DEC_SEQ // QB
    base = TP // QB

    def kv_spec(off):
        return pl.BlockSpec((None, QB, dh), lambda b, h_, i: (h_, base + b * nq + jnp.clip(i + off, 0, nq - 1), 0))

    cx = pl.BlockSpec((None, None, PAST_LEN, dh), lambda b, h_, i: (b, h_, 0, 0))
    o_lat = pl.pallas_call(
        _attn_lat_body,
        out_shape=jax.ShapeDtypeStruct((TS, NQ_COLS), BF16),
        grid=(DEC_BATCH, KVH, nq),
        in_specs=[smem,
                  pl.BlockSpec((None, None, G, QB, dh), lambda b, h_, i: (base + b * nq + i, h_, 0, 0, 0)),
                  kv_spec(-1), kv_spec(0), kv_spec(1), kv_spec(-1), kv_spec(0), kv_spec(1), cx, cx],
        out_specs=pl.BlockSpec((QB, G * dh), lambda b, h_, i: (b * nq + i, h_)),
        compiler_params=_cp("arbitrary", "arbitrary", "arbitrary"),
        name="attn_lat")(sink, q5, k3, k3, k3, v3, v3, v3,
                         jnp.transpose(cache_k, (0, 2, 1, 3)), jnp.transpose(cache_v, (0, 2, 1, 3)))
    return jnp.concatenate([o_ctx, o_lat], axis=0), z


NA_SCALE = NA_HEAD_DIM ** -0.5
NA_QROWS = 8
NA_KROWS = 16
NA_RB = NA_QROWS * GRID_W
NA_KB = NA_KROWS * GRID_W
NA_R = DEC_SEQ // GRID_W
NA_NBLK = NA_R // NA_QROWS
NA_RPB_W = 2 * NA_COLS - 1
NA_RPB_SIZE = (2 * NA_ROWS - 1) * NA_RPB_W
NA_CTX_HEADS_PER_STEP = 4


def _na_key_row0(a):
    return min(max(NA_QROWS * a - NA_ROWS // 2, 0), NA_R - NA_KROWS)


def _na_bias_body(rpb_ref, o_ref):
    h = pl.program_id(0)
    cq = lax.broadcasted_iota(I32, (GRID_W, 128), 0)
    ck = lax.broadcasted_iota(I32, (GRID_W, 128), 1) % GRID_W
    c0 = jnp.clip(cq - NA_COLS // 2, 0, GRID_W - NA_COLS)
    col_ok = (ck >= c0) & (ck < c0 + NA_COLS)
    dc = jnp.clip(ck - cq + NA_COLS - 1, 0, NA_RPB_W - 1)
    neg = jnp.full((GRID_W, 128), NEG_INF, F32)
    tiles = []
    for dr in range(2 * NA_ROWS - 1):
        t = neg
        for j in range(NA_RPB_W):
            t = jnp.where(dc == j, rpb_ref[h * NA_RPB_SIZE + dr * NA_RPB_W + j], t)
        tiles.append(jnp.where(col_ok, t, NEG_INF))
    left = lax.broadcasted_iota(I32, (GRID_W, 128), 1) < GRID_W
    for var, a in enumerate((0, 1, NA_NBLK - 1)):
        kr0 = _na_key_row0(a)
        for rq in range(NA_QROWS):
            r = NA_QROWS * a + rq
            r0 = min(max(r - NA_ROWS // 2, 0), NA_R - NA_ROWS)
            for p in range(NA_KROWS // 2):
                halves = []
                for kr in (kr0 + 2 * p, kr0 + 2 * p + 1):
                    halves.append(tiles[kr - r + NA_ROWS - 1] if r0 <= kr < r0 + NA_ROWS else neg)
                o_ref[var, rq * GRID_W:(rq + 1) * GRID_W, p * 128:(p + 1) * 128] = (
                    halves[0] if halves[0] is halves[1] else jnp.where(left, halves[0], halves[1]))


def _na_bias(rpb):
    return pl.pallas_call(
        _na_bias_body,
        out_shape=jax.ShapeDtypeStruct((NA_HEADS, 3, NA_RB, NA_KB), F32),
        grid=(NA_HEADS,),
        in_specs=[pl.BlockSpec(memory_space=pltpu.SMEM)],
        out_specs=pl.BlockSpec((None, 3, NA_RB, NA_KB), lambda h_: (h_, 0, 0, 0)),
        compiler_params=_cp("arbitrary"),
        name="na_bias")(rpb.reshape(-1))


def _na_ctx_body(q_ref, k_ref, v_ref, o_ref):
    for hh in range(NA_CTX_HEADS_PER_STEP):
        sl = slice(hh * NA_HEAD_DIM, (hh + 1) * NA_HEAD_DIM)
        s = _dot_nt(q_ref[:, sl].astype(BF16), k_ref[:, sl].astype(BF16)) * NA_SCALE
        o_ref[:, sl] = _softmax_pv([s], [v_ref[:, sl].astype(BF16)]).astype(BF16)


def _na_lat_body(q_ref, k_ref, v_ref, bias_ref, kx_ref, vx_ref, o_ref):
    a = pl.program_id(2)
    k0 = pl.multiple_of(jnp.clip(NA_QROWS * a - NA_ROWS // 2, 0, NA_R - NA_KROWS) * GRID_W, 256)
    q = q_ref[...].astype(BF16)
    s_loc = _dot_nt(q, k_ref[pl.ds(k0, NA_KB), :].astype(BF16)) * NA_SCALE + bias_ref[...]
    s_ctx = _dot_nt(q, kx_ref[...].astype(BF16)) * NA_SCALE
    o = _softmax_pv([s_loc, s_ctx], [v_ref[pl.ds(k0, NA_KB), :].astype(BF16), vx_ref[...].astype(BF16)])
    o_ref[...] = o.astype(BF16)


def _na_mixer(h, wqkv, rpb, cache_k, cache_v):
    z = _matmul(h, wqkv, 0, name="na_qkv")
    nh, dh = NA_HEADS, NA_HEAD_DIM
    hps = NA_CTX_HEADS_PER_STEP
    cw = hps * dh
    o_ctx = pl.pallas_call(
        _na_ctx_body,
        out_shape=jax.ShapeDtypeStruct((TP, nh * dh), BF16),
        grid=(BATCH, nh // hps),
        in_specs=[pl.BlockSpec((SEQ, cw), lambda b, g: (b, g)),
                  pl.BlockSpec((SEQ, cw), lambda b, g: (b, nh // hps + g)),
                  pl.BlockSpec((SEQ, cw), lambda b, g: (b, 2 * (nh // hps) + g))],
        out_specs=pl.BlockSpec((SEQ, cw), lambda b, g: (b, g)),
        compiler_params=_cp("arbitrary", "arbitrary"),
        name="na_ctx")(z, z, z)

    bias = _na_bias(rpb)
    qbase = TP // NA_RB
    sbase = TP // DEC_SEQ

    def var(a):
        return jnp.where(a == 0, 0, jnp.where(a == NA_NBLK - 1, 2, 1))

    cx = pl.BlockSpec((None, PAST_LEN, dh), lambda b, h_, a: (b, 0, h_))
    o_lat = pl.pallas_call(
        _na_lat_body,
        out_shape=jax.ShapeDtypeStruct((TS, nh * dh), BF16),
        grid=(DEC_BATCH, nh, NA_NBLK),
        in_specs=[pl.BlockSpec((NA_RB, dh), lambda b, h_, a: (qbase + b * NA_NBLK + a, h_)),
                  pl.BlockSpec((DEC_SEQ, dh), lambda b, h_, a: (sbase + b, nh + h_)),
                  pl.BlockSpec((DEC_SEQ, dh), lambda b, h_, a: (sbase + b, 2 * nh + h_)),
                  pl.BlockSpec((None, None, NA_RB, NA_KB), lambda b, h_, a: (h_, var(a), 0, 0)),
                  cx, cx],
        out_specs=pl.BlockSpec((NA_RB, dh), lambda b, h_, a: (b * NA_NBLK + a, h_)),
        compiler_params=_cp("arbitrary", "arbitrary", "arbitrary"),
        name="na_lat")(z, z, z, bias, cache_k.reshape(DEC_BATCH, PAST_LEN, nh * dh),
                       cache_v.reshape(DEC_BATCH, PAST_LEN, nh * dh))
    return jnp.concatenate([o_ctx, o_lat], axis=0), z


ML = 256
MLSTM_QSCALE = MLSTM_DK ** -0.5
NQK = MLSTM_HEADS * MLSTM_DK
NV = MLSTM_HEADS * MLSTM_DV


def _log_sigmoid(x):
    return jnp.minimum(x, 0.0) - jnp.log(1.0 + jnp.exp(-jnp.abs(x)))


def _mlstm_chunk(d, q, k, v, grow, gcol, state):
    sgn = 1 - 2 * d
    ti = lax.broadcasted_iota(I32, (ML, ML), 0)
    si = lax.broadcasted_iota(I32, (ML, ML), 1)
    before = (si - ti) * sgn <= 0
    before_t = (ti - si) * sgn <= 0
    i_row, f_row = grow[0:1], _log_sigmoid(grow[1:2])
    i_col, f_col = gcol[:, 0:1], _log_sigmoid(gcol[:, 1:2])
    b_col = jnp.sum(jnp.where(before, f_row, 0.0), axis=1, keepdims=True)
    b_row = jnp.sum(jnp.where(before_t, f_col, 0.0), axis=0, keepdims=True)
    b_last = jnp.sum(f_row, axis=1, keepdims=True)
    r_row = i_row - b_row
    m_prev = state[2] if state is not None else jnp.zeros((1, 1), F32)
    mx_col = jnp.maximum(jnp.max(jnp.where(before, r_row, -jnp.inf), axis=1, keepdims=True), m_prev)
    w = jnp.exp(jnp.where(before, r_row - mx_col, -jnp.inf))
    qs = q * MLSTM_QSCALE
    qb, kb, vb = qs.astype(BF16), k.astype(BF16), v.astype(BF16)
    sqk = _dot_nt(qb, kb) * w
    num = jnp.dot(sqk.astype(BF16), vb, preferred_element_type=F32)
    den = jnp.sum(sqk, axis=1, keepdims=True)
    if state is not None:
        a_col = jnp.exp(m_prev - mx_col)
        num = num + a_col * jnp.dot(qb, state[0].astype(BF16), preferred_element_type=F32)
        den = den + a_col * jnp.sum(qs * state[1], axis=1, keepdims=True)
    h = num * (1.0 / jnp.maximum(jnp.abs(den), jnp.exp(-(b_col + mx_col))))
    g_col = b_last - b_col + i_col
    m_new = jnp.maximum(b_last + m_prev, jnp.max(g_col, axis=0, keepdims=True))
    kw = k * jnp.exp(g_col - m_new)
    c_new = lax.dot_general(kw.astype(BF16), vb, (((0,), (0,)), ((), ())), preferred_element_type=F32)
    n_new = jnp.sum(kw, axis=0, keepdims=True)
    if state is not None:
        decay = jnp.exp(b_last + m_prev - m_new)
        c_new = decay * state[0] + c_new
        n_new = decay * state[1] + n_new
    return h, (c_new, n_new, m_new)


def _mlstm_ctx_body(q_ref, k_ref, v_ref, grow_ref, gcol_ref, h_ref, c_ref, n_ref, m_ref):
    d = pl.program_id(2)
    h, (c, n, m) = _mlstm_chunk(d, q_ref[...], k_ref[...], v_ref[...], grow_ref[...], gcol_ref[...], None)
    h_ref[...] = h
    c_ref[...] = c
    n_ref[...] = n
    m_ref[...] = jnp.broadcast_to(m, m_ref.shape)


def _mlstm_lat_body(m0_ref, q_ref, k_ref, v_ref, grow_ref, gcol_ref, c0_ref, n0_ref, h_ref, c_s, n_s, m_s):
    b, hd, d, c = pl.program_id(0), pl.program_id(1), pl.program_id(2), pl.program_id(3)

    @pl.when(c == 0)
    def _():
        c_s[...] = c0_ref[...]
        n_s[...] = n0_ref[...]
        m_s[...] = jnp.full(m_s.shape, m0_ref[(b * 2 + d) * MLSTM_HEADS + hd], F32)

    state = (c_s[...], n_s[...], m_s[:, 0:1])
    h, (cn, nn, mn) = _mlstm_chunk(d, q_ref[...], k_ref[...], v_ref[...], grow_ref[...], gcol_ref[...], state)
    h_ref[...] = h
    c_s[...] = cn
    n_s[...] = nn
    m_s[...] = jnp.broadcast_to(mn, m_s.shape)


def _mlstm_out_body(hf_ref, hb_ref, o_ref, gn_ref, y_ref):
    for hd in range(MLSTM_HEADS):
        sl = slice(hd * MLSTM_DV, (hd + 1) * MLSTM_DV)
        hs = hf_ref[:, sl] + hb_ref[:, sl]
        hn = hs * lax.rsqrt(jnp.mean(hs * hs, axis=-1, keepdims=True) + LN_EPS) * gn_ref[:, sl]
        y_ref[:, sl] = (jax.nn.sigmoid(o_ref[:, sl]) * hn).astype(BF16)


def _mlstm_mixer(h, win, wgate, bgate, gnorm, state_c, state_n, state_m):
    H, DK, DV = MLSTM_HEADS, MLSTM_DK, MLSTM_DV
    z = _matmul(h, win, 0, name="mlstm_in")
    g = _matmul(h, wgate, 0, bgate, tm=ROW_TILE, name="mlstm_gate")
    g4 = g.reshape(T, 2, 2, H)
    grow = jnp.transpose(g4, (1, 3, 2, 0))
    gcol = jnp.transpose(g4, (1, 3, 0, 2))

    def specs(row_blk):
        return [pl.BlockSpec((ML, DK), lambda *a: (row_blk(*a), a[1])),
                pl.BlockSpec((ML, DK), lambda *a: (row_blk(*a), H + a[1])),
                pl.BlockSpec((ML, DV), lambda *a: (row_blk(*a), H + a[1])),
                pl.BlockSpec((None, None, 2, ML), lambda *a: (a[2], a[1], 0, row_blk(*a))),
                pl.BlockSpec((None, None, ML, 2), lambda *a: (a[2], a[1], row_blk(*a), 0))]

    def hspec(row_blk):
        return pl.BlockSpec((None, ML, DV), lambda *a: (a[2], row_blk(*a), a[1]))

    assert SEQ == ML
    st = lambda shape: pl.BlockSpec((None, None, None) + shape, lambda s, hd, d: (s, d, hd, 0, 0))
    h_ctx, c_new, n_new, m_new = pl.pallas_call(
        _mlstm_ctx_body,
        out_shape=(jax.ShapeDtypeStruct((2, TP, NV), F32),
                   jax.ShapeDtypeStruct((BATCH, 2, H, DK, DV), F32),
                   jax.ShapeDtypeStruct((BATCH, 2, H, 1, DK), F32),
                   jax.ShapeDtypeStruct((BATCH, 2, H, 1, 128), F32)),
        grid=(BATCH, H, 2),
        in_specs=specs(lambda s, hd, d: s),
        out_specs=(hspec(lambda s, hd, d: s), st((DK, DV)), st((1, DK)), st((1, 128))),
        compiler_params=_cp("arbitrary", "arbitrary", "arbitrary"),
        name="mlstm_ctx")(z, z, z, grow, gcol)

    nc = DEC_SEQ // ML
    base = TP // ML

    def lat_blk(b, hd, d, c):
        return base + b * nc + jnp.where(d == 0, c, nc - 1 - c)

    lst = lambda shape: pl.BlockSpec((None, None, None) + shape, lambda b, hd, d, c: (b, d, hd, 0, 0))
    h_lat = pl.pallas_call(
        _mlstm_lat_body,
        out_shape=jax.ShapeDtypeStruct((2, TS, NV), F32),
        grid=(DEC_BATCH, H, 2, nc),
        in_specs=[pl.BlockSpec(memory_space=pltpu.SMEM)] + specs(lat_blk) + [lst((DK, DV)), lst((1, DK))],
        out_specs=pl.BlockSpec((None, ML, DV), lambda b, hd, d, c: (d, lat_blk(b, hd, d, c) - base, hd)),
        scratch_shapes=[pltpu.VMEM((DK, DV), F32), pltpu.VMEM((1, DK), F32), pltpu.VMEM((1, 128), F32)],
        compiler_params=_cp("arbitrary", "arbitrary", "arbitrary", "arbitrary"),
        name="mlstm_lat")(state_m.reshape(-1), z, z, z, grow, gcol, state_c,
                          state_n.reshape(DEC_BATCH, 2, H, 1, DK))

    hh = jnp.concatenate([h_ctx, h_lat], axis=1)
    row = lambda j: pl.BlockSpec((ROW_TILE, NV), lambda i: (i, j))
    y = pl.pallas_call(
        _mlstm_out_body,
        out_shape=jax.ShapeDtypeStruct((T, NV), BF16),
        grid=(T // ROW_TILE,),
        in_specs=[pl.BlockSpec((None, ROW_TILE, NV), lambda i: (0, i, 0)),
                  pl.BlockSpec((None, ROW_TILE, NV), lambda i: (1, i, 0)),
                  row((2 * NQK + NV) // NV), pl.BlockSpec((1, NV), lambda i: (0, 0))],
        out_specs=row(0),
        compiler_params=_cp("arbitrary"),
        name="mlstm_out")(hh, hh, z, gnorm.reshape(1, NV))
    new_state = (c_new, n_new.reshape(BATCH, 2, H, DK), m_new[:, :, :, 0, 0])
    return y, new_state


CONV_HALO = 16
CONV_RT = 64
assert SEQ == ROW_TILE and DEC_SEQ % ROW_TILE == 0


def _conv_body(a_ref, g_ref, ap_ref, gp_ref, an_ref, gn_ref, dw_ref, dwb_ref, lng_ref, lnb_ref, y_ref, u_ref, c_ref):
    i = pl.program_id(0)
    tiles_per_seq = DEC_SEQ // ROW_TILE
    pos = (i - TP // ROW_TILE) % tiles_per_seq
    is_lat = i >= TP // ROW_TILE
    has_prev = is_lat & (pos > 0)
    has_next = is_lat & (pos < tiles_per_seq - 1)

    def glu(a, g):
        return a * jax.nn.sigmoid(g)

    H = CONV_HALO
    u_ref[0:H, :] = jnp.where(has_prev, glu(ap_ref[...], gp_ref[...]), 0.0)
    u_ref[H:H + ROW_TILE, :] = glu(a_ref[...], g_ref[...])
    u_ref[H + ROW_TILE:, :] = jnp.where(has_next, glu(an_ref[...], gn_ref[...]), 0.0)

    off = H - CONV_WIDTH // 2

    def lane_chunk(c, carry):
        c0 = pl.multiple_of(c * 128, 128)
        for r in range(ROW_TILE // CONV_RT):
            acc = jnp.zeros((CONV_RT, 128), F32)
            for j in range(CONV_WIDTH):
                acc = acc + u_ref[pl.ds(r * CONV_RT + off + j, CONV_RT), pl.ds(c0, 128)] * dw_ref[j:j + 1, pl.ds(c0, 128)]
            c_ref[pl.ds(r * CONV_RT, CONV_RT), pl.ds(c0, 128)] = acc
        return carry

    lax.fori_loop(0, D_MODEL // 128, lane_chunk, 0)
    v = c_ref[...] + dwb_ref[...]
    mu = jnp.mean(v, axis=-1, keepdims=True)
    d = v - mu
    var = jnp.mean(d * d, axis=-1, keepdims=True)
    xn = d * lax.rsqrt(var + LN_EPS) * lng_ref[...] + lnb_ref[...]
    y_ref[...] = (xn * jax.nn.sigmoid(xn)).astype(BF16)


def _conv_mixer(h, w1, b1, dw, dwb, lng, lnb):
    z = _matmul(h, w1, 0, b1, name="conv_in")
    D = D_MODEL
    hb = ROW_TILE // CONV_HALO
    nh = T // CONV_HALO
    cur = lambda j: pl.BlockSpec((ROW_TILE, D), lambda i: (i, j))
    prev = lambda j: pl.BlockSpec((CONV_HALO, D), lambda i: (jnp.maximum(i * hb - 1, 0), j))
    nxt = lambda j: pl.BlockSpec((CONV_HALO, D), lambda i: (jnp.minimum((i + 1) * hb, nh - 1), j))
    vec = pl.BlockSpec((None, 1, D), lambda i: (0, 0, 0))
    return pl.pallas_call(
        _conv_body,
        out_shape=jax.ShapeDtypeStruct((T, D), BF16),
        grid=(T // ROW_TILE,),
        in_specs=[cur(0), cur(1), prev(0), prev(1), nxt(0), nxt(1),
                  pl.BlockSpec((None, CONV_WIDTH, D), lambda i: (0, 0, 0)), vec, vec, vec],
        out_specs=cur(0),
        scratch_shapes=[pltpu.VMEM((ROW_TILE + 2 * CONV_HALO, D), F32), pltpu.VMEM((ROW_TILE, D), F32)],
        compiler_params=_cp("arbitrary"),
        name="conv_mid")(z, z, z, z, z, z, dw, dwb.reshape(-1, 1, D), lng.reshape(-1, 1, D), lnb.reshape(-1, 1, D))


MOE_TM = 512
MOE_NP = T * TOP_K // (2 * MOE_TM) + N_EXPERTS
MOE_NT = 2 * MOE_NP
MOE_UC = 512
MOE_TC = MOE_UC // 2
MOE_N1 = D_EXPERT // MOE_TC
PERM_W = 256
MOE_TN = 1024
MOE_N2 = D_MODEL // MOE_TN
MOE_CT = 128
TOPK_TILE = 1024


def _topk_body(l_ref, idx_ref, gate_ref):
    l = l_ref[...]
    lane = lax.broadcasted_iota(I32, l.shape, 1)
    vals, idxs = [], []
    for _ in range(TOP_K):
        m = jnp.max(l, axis=-1, keepdims=True)
        ix = jnp.min(jnp.where(l == m, lane, N_EXPERTS), axis=-1, keepdims=True)
        vals.append(m)
        idxs.append(ix)
        l = jnp.where(lane == ix, -jnp.inf, l)
    e = [jnp.exp(v - vals[0]) for v in vals]
    inv = 1.0 / functools.reduce(jnp.add, e)
    kcol = lax.broadcasted_iota(I32, idx_ref.shape, 1)
    idx = jnp.zeros(idx_ref.shape, I32)
    gate = jnp.zeros(gate_ref.shape, F32)
    for k in range(TOP_K):
        idx = jnp.where(kcol == k, idxs[k], idx)
        gate = jnp.where(kcol == k, e[k] * inv, gate)
    idx_ref[...] = idx
    gate_ref[...] = gate


def _topk(logits):
    spec = lambda w: pl.BlockSpec((TOPK_TILE, w), lambda i: (i, 0))
    return pl.pallas_call(
        _topk_body,
        out_shape=(jax.ShapeDtypeStruct((T, TOP_K), I32), jax.ShapeDtypeStruct((T, TOP_K), F32)),
        grid=(T // TOPK_TILE,),
        in_specs=[spec(N_EXPERTS)],
        out_specs=(spec(TOP_K), spec(TOP_K)),
        compiler_params=_cp("arbitrary"),
        name="moe_topk")(logits)


def _moe_plan(top_i):
    A = T * TOP_K
    flat_e = top_i.reshape(A)
    onehot = (flat_e[:, None] == jnp.arange(N_EXPERTS, dtype=I32)[None, :]).astype(I32)
    rank = jnp.take_along_axis(jnp.cumsum(onehot, axis=0), flat_e[:, None], axis=1)[:, 0] - 1
    counts = jnp.sum(onehot, axis=0)
    n_tiles = (counts + MOE_TM - 1) // MOE_TM
    padded = (n_tiles + 1) // 2 * (2 * MOE_TM)
    pad_end = jnp.cumsum(padded)
    pad_start = pad_end - padded
    dest = pad_start[flat_e] + rank
    slot_tok = jnp.zeros((MOE_NT * MOE_TM,), I32).at[dest].set(jnp.arange(A, dtype=I32) // TOP_K)
    pair_start = jnp.arange(MOE_NP, dtype=I32) * (2 * MOE_TM)
    pair_e = jnp.minimum(jnp.searchsorted(pad_end, pair_start, side='right'), N_EXPERTS - 1).astype(I32)
    tile_start = jnp.arange(MOE_NT, dtype=I32) * MOE_TM
    tile_e = jnp.repeat(pair_e, 2)
    tile_on = ((tile_start < (pad_start + n_tiles * MOE_TM)[tile_e]) & (tile_start < pad_end[-1])).astype(I32)
    return slot_tok, pair_e, tile_on, dest.astype(I32)


def _deinterleave(u):
    r = lax.broadcasted_iota(I32, (PERM_W, PERM_W), 0)
    c = lax.broadcasted_iota(I32, (PERM_W, PERM_W), 1)
    half = PERM_W // 2
    src = jnp.where(c < half, 2 * c, 2 * (c - half) + 1)
    sel = jnp.where(r == src, 1.0, 0.0).astype(BF16)
    hi = u.astype(BF16)
    r1 = u - hi.astype(F32)
    mid = r1.astype(BF16)
    lo = (r1 - mid.astype(F32)).astype(BF16)
    out = (jnp.dot(hi, sel, preferred_element_type=F32) + jnp.dot(mid, sel, preferred_element_type=F32)
           + jnp.dot(lo, sel, preferred_element_type=F32))
    return out[:, :half], out[:, half:]


def _gmm_body(te_ref, on_ref, tok_ref, tokn_ref, x_hbm, w1_ref, b1_ref, w2_ref, b2_ref,
              y_ref, xg_ref, xbf_ref, act_ref, sem):
    j, s = pl.program_id(1), pl.program_id(2)
    t = 2 * pl.program_id(0) + s
    on = on_ref[t] != 0

    def row_copy(src_row, r):
        return pltpu.make_async_copy(x_hbm.at[pl.ds(src_row, 1), :], xg_ref.at[pl.ds(r, 1), :], sem)

    def issue_rows(tk_ref):
        def issue(r, c):
            row_copy(tk_ref[0, r], r).start()
            return c

        lax.fori_loop(0, MOE_TM, issue, 0, unroll=8)

    @pl.when(on & (j == 0) & (t == 0))
    def _():
        issue_rows(tok_ref)

    @pl.when(on & (j == 0))
    def _():
        def wait(r, c):
            row_copy(0, r).wait()
            return c

        lax.fori_loop(0, MOE_TM, wait, 0, unroll=8)
        xbf_ref[s] = xg_ref[...].astype(BF16)

    nxt = jnp.minimum(t + 1, MOE_NT - 1)

    @pl.when((j == 0) & (t + 1 < MOE_NT) & (on_ref[nxt] != 0))
    def _():
        issue_rows(tokn_ref)

    for jj in range(MOE_N1):
        @pl.when(on & (j == jj))
        def _():
            u = jnp.dot(xbf_ref[s], w1_ref[...].astype(BF16), preferred_element_type=F32) + b1_ref[...]
            acts = []
            for blk in range(MOE_UC // PERM_W):
                glu, lin = _deinterleave(u[:, blk * PERM_W:(blk + 1) * PERM_W])
                glu = jnp.minimum(glu, SWIGLU_LIMIT)
                lin = jnp.clip(lin, -SWIGLU_LIMIT, SWIGLU_LIMIT)
                acts.append((glu * jax.nn.sigmoid(SWIGLU_ALPHA * glu) * (lin + 1.0)).astype(BF16))
            act_ref[s, jj] = jnp.concatenate(acts, axis=1)

    @pl.when(on & (j >= MOE_N1))
    def _():
        acc = jnp.broadcast_to(b2_ref[...], y_ref.shape)
        for jj in range(MOE_N1):
            w2 = w2_ref[jj * MOE_TC:(jj + 1) * MOE_TC, :].astype(BF16)
            acc = acc + jnp.dot(act_ref[s, jj], w2, preferred_element_type=F32)
        y_ref[...] = acc

    @pl.when(jnp.logical_not(on) & (j >= MOE_N1))
    def _():
        y_ref[...] = jnp.zeros(y_ref.shape, F32)


def _gmm(x, slot_tok, tile_e, tile_on, w1, b1, w2, b2, lidx):
    D, DE = D_MODEL, D_EXPERT
    c1 = lambda j: jnp.minimum(j, MOE_N1 - 1)
    c2 = lambda j: jnp.maximum(j - MOE_N1, 0)
    tok3 = slot_tok.reshape(MOE_NT, 1, MOE_TM)
    tok_spec = lambda nxt: pl.BlockSpec((None, 1, MOE_TM),
                                        lambda p, j, s, te, on: (jnp.minimum(2 * p + s + nxt, MOE_NT - 1), 0, 0),
                                        memory_space=pltpu.SMEM)
    grid_spec = pltpu.PrefetchScalarGridSpec(
        num_scalar_prefetch=2,
        grid=(MOE_NP, MOE_N1 + MOE_N2, 2),
        in_specs=[tok_spec(0), tok_spec(1),
                  pl.BlockSpec(memory_space=pl.ANY),
                  pl.BlockSpec((None, None, D, MOE_UC), lambda p, j, s, te, on: (lidx, te[p], 0, c1(j))),
                  pl.BlockSpec((None, None, 1, MOE_UC), lambda p, j, s, te, on: (lidx, te[p], 0, c1(j))),
                  pl.BlockSpec((None, None, DE, MOE_TN), lambda p, j, s, te, on: (lidx, te[p], 0, c2(j))),
                  pl.BlockSpec((None, None, 1, MOE_TN), lambda p, j, s, te, on: (lidx, te[p], 0, c2(j)))],
        out_specs=pl.BlockSpec((MOE_TM, MOE_TN),
                               lambda p, j, s, te, on: (2 * p + jnp.where(j < MOE_N1, 0, s), c2(j))),
        scratch_shapes=[pltpu.VMEM((MOE_TM, D), F32), pltpu.VMEM((2, MOE_TM, D), BF16),
                        pltpu.VMEM((2, MOE_N1, MOE_TM, MOE_TC), BF16), pltpu.SemaphoreType.DMA(())])
    return pl.pallas_call(
        _gmm_body,
        out_shape=jax.ShapeDtypeStruct((MOE_NT * MOE_TM, D), F32),
        grid_spec=grid_spec,
        compiler_params=_cp("arbitrary", "arbitrary", "arbitrary"),
        name="moe_gmm")(tile_e, tile_on, tok3, tok3, x, w1, b1.reshape(b1.shape[0], N_EXPERTS, 1, 2 * DE),
                        w2, b2.reshape(b2.shape[0], N_EXPERTS, 1, D))


def _combine_body(slot_ref, gate_ref, y_hbm, o_ref, buf_ref, sem):
    def row_copy(r, k):
        return pltpu.make_async_copy(y_hbm.at[pl.ds(slot_ref[0, r * TOP_K + k], 1), :],
                                     buf_ref.at[k, pl.ds(r, 1), :], sem)

    def issue(r, c):
        for k in range(TOP_K):
            row_copy(r, k).start()
        return c

    lax.fori_loop(0, MOE_CT, issue, 0)

    def wait(r, c):
        for k in range(TOP_K):
            row_copy(r, k).wait()
        return c

    lax.fori_loop(0, MOE_CT, wait, 0)
    g = gate_ref[...]
    acc = buf_ref[0] * g[:, 0:1]
    for k in range(1, TOP_K):
        acc = acc + buf_ref[k] * g[:, k:k + 1]
    o_ref[...] = acc


def _combine(yb, slot_of, gates):
    n = T // MOE_CT
    return pl.pallas_call(
        _combine_body,
        out_shape=jax.ShapeDtypeStruct((T, D_MODEL), F32),
        grid=(n,),
        in_specs=[pl.BlockSpec((None, 1, MOE_CT * TOP_K), lambda i: (i, 0, 0), memory_space=pltpu.SMEM),
                  pl.BlockSpec((MOE_CT, TOP_K), lambda i: (i, 0)),
                  pl.BlockSpec(memory_space=pl.ANY)],
        out_specs=pl.BlockSpec((MOE_CT, D_MODEL), lambda i: (i, 0)),
        scratch_shapes=[pltpu.VMEM((TOP_K, MOE_CT, D_MODEL), F32), pltpu.SemaphoreType.DMA(())],
        compiler_params=_cp("arbitrary"),
        name="moe_combine")(slot_of.reshape(n, 1, MOE_CT * TOP_K), gates, yb)


def _moe(h, wr, br, w1, b1, w2, b2, lidx):
    logits = _matmul(h, wr, lidx, br, tm=ROW_TILE, name="moe_router")
    top_i, gates = _topk(logits)
    slot_tok, tile_e, tile_on, slot_of = _moe_plan(top_i)
    yb = _gmm(h, slot_tok, tile_e, tile_on, w1, b1, w2, b2, lidx)
    return _combine(yb, slot_of, gates)


def kernel(x_prompt, x_sample, cache_attn_k, cache_attn_v, cache_na_k, cache_na_v, state_mlstm_C, state_mlstm_n, state_mlstm_m, c, c_ctx, ada_w, ada_b, ln1_g, ln1_b, ln2_g, ln2_b, attn_wqkv, attn_wo, attn_sink, na_wqkv, na_wo, na_rpb, mlstm_win, mlstm_wgate, mlstm_bgate, mlstm_gnorm, mlstm_wo, conv_w1, conv_b1, conv_dw, conv_dwb, conv_ln_g, conv_ln_b, conv_w2, conv_b2, moe_wr, moe_br, moe_w1, moe_b1, moe_w2, moe_b2):
    D = D_MODEL
    x = jnp.concatenate([x_prompt.reshape(TP, D), x_sample.reshape(TS, D)], axis=0)
    cond = jnp.concatenate([c_ctx[None, :], c, jnp.zeros((ADA_ROWS - 1 - DEC_BATCH, D), F32)], axis=0)
    mods = _ada(cond, ada_w, ada_b)[:, :N_SEG].reshape(DEPTH, N_SEG, 6, D)

    h = _modulate(x, mods[0])
    outs = {}
    for i in range(DEPTH):
        mixer = i % 4
        if mixer == 0:
            o, z = _attn_mixer(h, attn_wqkv, attn_sink[0], cache_attn_k[:, 0], cache_attn_v[:, 0])
            kv = z[:TP, NQ_COLS:].reshape(BATCH, 1, SEQ, 2, ATTN_KV_HEADS, ATTN_HEAD_DIM)
            outs['ak'], outs['av'] = kv[:, :, :, 0], kv[:, :, :, 1]
            y = _matmul(o, attn_wo, 0, name="attn_out")
        elif mixer == 1:
            o, z = _na_mixer(h, na_wqkv, na_rpb[0], cache_na_k[:, 0], cache_na_v[:, 0])
            kv = z[:TP, NA_HEADS * NA_HEAD_DIM:].reshape(BATCH, 1, SEQ, 2, NA_HEADS, NA_HEAD_DIM)
            outs['nk'], outs['nv'] = kv[:, :, :, 0], kv[:, :, :, 1]
            y = _matmul(o, na_wo, 0, name="na_out")
        elif mixer == 2:
            o, st = _mlstm_mixer(h, mlstm_win, mlstm_wgate, mlstm_bgate, mlstm_gnorm[0],
                                 state_mlstm_C[:, 0], state_mlstm_n[:, 0], state_mlstm_m[:, 0])
            outs['C'], outs['n'], outs['m'] = (s[:, None] for s in st)
            y = _matmul(o, mlstm_wo, 0, name="mlstm_out_proj")
        else:
            o = _conv_mixer(h, conv_w1, conv_b1, conv_dw[0:1], conv_dwb[0:1], conv_ln_g[0:1], conv_ln_b[0:1])
            y = _matmul(o, conv_w2, 0, conv_b2, name="conv_out")
        x, h2 = _ln_mod(x, y, mods[i], ln1_g, ln1_b, i, 2, mods[i], 3, h_dtype=F32)
        f = _moe(h2, moe_wr, moe_br, moe_w1, moe_b1, moe_w2, moe_b2, i)
        if i + 1 < DEPTH:
            x, h = _ln_mod(x, f, mods[i], ln2_g, ln2_b, i, 5, mods[i + 1], 0)
        else:
            x = _ln_mod(x, f, mods[i], ln2_g, ln2_b, i, 5)
    return (x[:TP].reshape(BATCH, SEQ, D), x[TP:].reshape(DEC_BATCH, DEC_SEQ, D),
            outs['ak'], outs['av'], outs['nk'], outs['nv'], outs['C'], outs['n'], outs['m'])
```
